```python
import jax, jax.numpy as jnp
from jax import lax
import numpy as np

D_MODEL = 1024
BATCH = 32
SEQ = 2048
DEPTH = 1
DEC_BATCH = 2
DEC_SEQ = 16384
PAST_LEN = 128

MIX_WIDTH = D_MODEL
A_WIDTH = MIX_WIDTH // 2
B_WIDTH = MIX_WIDTH - A_WIDTH
A_GROUPS = 8
A_GROUP_DIM = A_WIDTH // A_GROUPS
CHUNK = 128
B_HEADS = 8
HEAD_DIM = B_WIDTH // B_HEADS
DILATED_PATTERNS = ((128, 1), (512, 4), (2048, 16))
IN_WIDTH = 2 * A_WIDTH + 3 * B_WIDTH
N_GROUPS = 4
EXPERTS_PER_GROUP = 8
N_EXPERTS = N_GROUPS * EXPERTS_PER_GROUP
TOP_K = 2
D_EXPERT = D_MODEL // 2
ROW_BLOCK = 256
EPS = 1e-6
NEG_INF = -1e30

kernel_name = 'hybrid_gmlp_dilated_hmoe_encoder'


def rms_norm(x, gain):
    xf = x.astype(jnp.float32)
    y = xf * lax.rsqrt(jnp.mean(xf * xf, axis=-1, keepdims=True) + EPS)
    return (y * gain.astype(jnp.float32)).astype(x.dtype)


def alibi_slopes(n_heads):
    return 2.0 ** (-8.0 * jnp.arange(1, n_heads + 1, dtype=jnp.float32) / n_heads)


def spatial_gating(pu, pv, v_gain, w_s, b_s):
    b, t, _ = pu.shape
    u = jax.nn.gelu(pu)
    v = rms_norm(jax.nn.gelu(pv).reshape(b, t, A_GROUPS, A_GROUP_DIM), v_gain)
    v = v.reshape(b, t // CHUNK, CHUNK, A_GROUPS, A_GROUP_DIM)
    mixed = jnp.einsum('gts,bnsgc->bntgc', w_s, v) + b_s.T[None, None, :, :, None]
    return u * mixed.reshape(b, t, A_WIDTH)


def dilated_window_branch(q, k, v, slopes, window, dil):
    b, t, h, hd = q.shape
    half = window // (2 * dil)
    blk = half
    n_sub = t // dil
    nblk = -(-n_sub // blk)
    lp = nblk * blk

    def to_sub(a):
        return jnp.swapaxes(a.reshape(b, n_sub, dil, h, hd), 1, 2)

    qs = jnp.pad(to_sub(q), ((0, 0), (0, 0), (0, lp - n_sub), (0, 0), (0, 0)))
    qs = qs.reshape(b, dil, nblk, blk, h, hd)
    kpad = ((0, 0), (0, 0), (blk, lp - n_sub + blk), (0, 0), (0, 0))
    ks = jnp.pad(to_sub(k), kpad).reshape(b, dil, nblk + 2, blk, h, hd)
    vs = jnp.pad(to_sub(v), kpad).reshape(b, dil, nblk + 2, blk, h, hd)
    kw = jnp.concatenate([ks[:, :, :-2], ks[:, :, 1:-1], ks[:, :, 2:]], axis=3)
    vw = jnp.concatenate([vs[:, :, :-2], vs[:, :, 1:-1], vs[:, :, 2:]], axis=3)

    s = jnp.einsum('brnqhc,brnkhc->brnhqk', qs, kw).astype(jnp.float32)
    qa = jnp.arange(blk)
    kc = jnp.arange(3 * blk)
    rel = kc[None, :] - blk - qa[:, None]
    kpos = jnp.arange(nblk)[:, None] * blk - blk + kc[None, :]
    valid = (jnp.abs(rel) <= half)[None] & ((kpos >= 0) & (kpos < n_sub))[:, None, :]
    bias = -slopes[:, None, None] * (jnp.abs(rel) * dil).astype(jnp.float32)[None]
    s = jnp.where(valid[:, None], s + bias, NEG_INF)
    m = jnp.max(s, axis=-1)
    p = jnp.exp(s - m[..., None])
    l = jnp.sum(p, axis=-1)
    o = jnp.einsum('brnhqk,brnkhc->brnqhc', p.astype(vw.dtype), vw).astype(jnp.float32)
    m = jnp.swapaxes(m, 3, 4)
    l = jnp.swapaxes(l, 3, 4)
    o = o / l[..., None]

    def from_sub(a):
        a = a.reshape((b, dil, lp) + a.shape[4:])[:, :, :n_sub]
        return jnp.swapaxes(a, 1, 2).reshape((b, t) + a.shape[3:])

    return from_sub(o), from_sub(m), from_sub(l)


def dilated_attention(q, k, v, q_gain, k_gain):
    b, t, _ = q.shape
    q = rms_norm(q.reshape(b, t, B_HEADS, HEAD_DIM), q_gain) * (HEAD_DIM ** -0.5)
    k = rms_norm(k.reshape(b, t, B_HEADS, HEAD_DIM), k_gain)
    v = v.reshape(b, t, B_HEADS, HEAD_DIM)
    slopes = alibi_slopes(B_HEADS)
    branches = [dilated_window_branch(q, k, v, slopes, w, d) for w, d in DILATED_PATTERNS]
    o = jnp.stack([br[0] for br in branches])
    m = jnp.stack([br[1] for br in branches])
    l = jnp.stack([br[2] for br in branches])
    wts = l * jnp.exp(m - jnp.max(m, axis=0, keepdims=True))
    out = jnp.sum(wts[..., None] * o, axis=0) / jnp.sum(wts, axis=0)[..., None]
    return out.reshape(b, t, B_WIDTH).astype(q.dtype)


def hierarchical_moe(x, w_group, b_group, w_expert_router, b_expert_router, w_gate, w_up, w_down):
    b, t, d = x.shape
    n = b * t
    xt = x.reshape(n, d)
    tok = jnp.arange(n)
    g_logits = (xt @ w_group + b_group).astype(jnp.float32)
    g_sel = jnp.argmax(g_logits, axis=-1)
    p_group = jax.nn.softmax(g_logits, axis=-1)[tok, g_sel]
    e_logits = (xt @ w_expert_router).astype(jnp.float32).reshape(n, N_GROUPS, EXPERTS_PER_GROUP)
    e_logits = e_logits + b_expert_router.astype(jnp.float32).reshape(N_GROUPS, EXPERTS_PER_GROUP)
    top_v, top_i = lax.top_k(e_logits[tok, g_sel], TOP_K)
    gates = p_group[:, None] * jax.nn.softmax(top_v, axis=-1)
    eid = g_sel[:, None] * EXPERTS_PER_GROUP + top_i

    n_assign = n * TOP_K
    flat_e = eid.reshape(n_assign)
    flat_tok = jnp.repeat(tok, TOP_K)
    flat_gate = gates.reshape(n_assign)
    order = jnp.argsort(flat_e)
    se, stok, sg = flat_e[order], flat_tok[order], flat_gate[order]
    counts = jnp.bincount(flat_e, length=N_EXPERTS)
    starts = jnp.cumsum(counts) - counts
    pcounts = (counts + ROW_BLOCK - 1) // ROW_BLOCK * ROW_BLOCK
    pends = jnp.cumsum(pcounts)
    pstarts = pends - pcounts
    dest = pstarts[se] + jnp.arange(n_assign) - starts[se]
    nb = -(-n_assign // ROW_BLOCK) + N_EXPERTS
    n_rows = nb * ROW_BLOCK
    row_tok = jnp.full((n_rows,), n, jnp.int32).at[dest].set(stok)
    row_gate = jnp.zeros((n_rows,), jnp.float32).at[dest].set(sg)
    xpad = jnp.concatenate([xt, jnp.zeros((1, d), xt.dtype)], axis=0)
    x_rows = xpad[row_tok].reshape(nb, ROW_BLOCK, d)
    block_e = jnp.minimum(jnp.searchsorted(pends, jnp.arange(nb) * ROW_BLOCK, side='right'), N_EXPERTS - 1)

    def expert_block(args):
        xb, e = args
        return (jax.nn.silu(xb @ w_gate[e]) * (xb @ w_up[e])) @ w_down[e]

    y_rows = lax.map(expert_block, (x_rows, block_e)).reshape(n_rows, d)
    out = jnp.zeros((n + 1, d), jnp.float32).at[row_tok].add(row_gate[:, None] * y_rows.astype(jnp.float32))
    return out[:n].reshape(b, t, d).astype(x.dtype)


def encoder_layer(x, norm_mix, w_in, a_v_norm, a_spatial_w, a_spatial_b, q_norm, k_norm,
                  out_norm_a, out_norm_b, w_out, norm_ffn, w_router_group, b_router_group,
                  w_router_expert, b_router_expert, w_expert_gate, w_expert_up, w_expert_down):
    h = rms_norm(x, norm_mix)
    proj = jnp.einsum('btd,de->bte', h, w_in)
    cuts = [A_WIDTH, 2 * A_WIDTH, 2 * A_WIDTH + B_WIDTH, 2 * A_WIDTH + 2 * B_WIDTH]
    pu, pv, q, k, v = jnp.split(proj, cuts, axis=-1)
    ya = rms_norm(spatial_gating(pu, pv, a_v_norm, a_spatial_w, a_spatial_b), out_norm_a)
    yb = rms_norm(dilated_attention(q, k, v, q_norm, k_norm), out_norm_b)
    x = x + jnp.einsum('bte,ed->btd', jnp.concatenate([ya, yb], axis=-1), w_out)
    x = x + hierarchical_moe(rms_norm(x, norm_ffn), w_router_group, b_router_group,
                             w_router_expert, b_router_expert, w_expert_gate, w_expert_up, w_expert_down)
    return x


def setup_inputs(seed: int = 0) -> dict:
    key = jax.random.key(seed)
    ks = jax.random.split(key, 20)
    f32 = jnp.float32

    def nrm(k, shape, scale):
        return jax.random.normal(k, shape, f32) * scale

    def gain(k, shape):
        return 1.0 + 0.01 * jax.random.normal(k, shape, f32)

    return {
        'x_prompt': nrm(ks[0], (BATCH, SEQ, D_MODEL), 1.0),
        'x_sample': nrm(ks[1], (DEC_BATCH, DEC_SEQ, D_MODEL), 1.0),
        'norm_mix': gain(ks[2], (DEPTH, D_MODEL)),
        'w_in': nrm(ks[3], (DEPTH, D_MODEL, IN_WIDTH), D_MODEL ** -0.5),
        'a_v_norm': gain(ks[4], (DEPTH, A_GROUPS, A_GROUP_DIM)),
        'a_spatial_w': nrm(ks[5], (DEPTH, A_GROUPS, CHUNK, CHUNK), CHUNK ** -0.5),
        'a_spatial_b': gain(ks[6], (DEPTH, A_GROUPS, CHUNK)),
        'q_norm': gain(ks[7], (DEPTH, HEAD_DIM)),
        'k_norm': gain(ks[8], (DEPTH, HEAD_DIM)),
        'out_norm_a': gain(ks[9], (DEPTH, A_WIDTH)),
        'out_norm_b': gain(ks[10], (DEPTH, B_WIDTH)),
        'w_out': nrm(ks[11], (DEPTH, MIX_WIDTH, D_MODEL), MIX_WIDTH ** -0.5),
        'norm_ffn': gain(ks[12], (DEPTH, D_MODEL)),
        'w_router_group': nrm(ks[13], (DEPTH, D_MODEL, N_GROUPS), D_MODEL ** -0.5),
        'b_router_group': nrm(ks[14], (DEPTH, N_GROUPS), 0.01),
        'w_router_expert': nrm(ks[15], (DEPTH, D_MODEL, N_EXPERTS), D_MODEL ** -0.5),
        'b_router_expert': nrm(ks[16], (DEPTH, N_EXPERTS), 0.01),
        'w_expert_gate': nrm(ks[17], (DEPTH, N_EXPERTS, D_MODEL, D_EXPERT), D_MODEL ** -0.5),
        'w_expert_up': nrm(ks[18], (DEPTH, N_EXPERTS, D_MODEL, D_EXPERT), D_MODEL ** -0.5),
        'w_expert_down': nrm(ks[19], (DEPTH, N_EXPERTS, D_EXPERT, D_MODEL), D_EXPERT ** -0.5),
    }


def reference(x_prompt, x_sample, norm_mix, w_in, a_v_norm, a_spatial_w, a_spatial_b, q_norm, k_norm,
              out_norm_a, out_norm_b, w_out, norm_ffn, w_router_group, b_router_group,
              w_router_expert, b_router_expert, w_expert_gate, w_expert_up, w_expert_down):
    def run(x):
        for l in range(DEPTH):
            x = encoder_layer(x, norm_mix[l], w_in[l], a_v_norm[l], a_spatial_w[l], a_spatial_b[l],
                              q_norm[l], k_norm[l], out_norm_a[l], out_norm_b[l], w_out[l], norm_ffn[l],
                              w_router_group[l], b_router_group[l], w_router_expert[l], b_router_expert[l],
                              w_expert_gate[l], w_expert_up[l], w_expert_down[l])
        return x

    y_prompt = run(x_prompt)
    y_sample = run(x_sample)
    return (y_prompt, y_sample)
```

```python
import functools

import jax
import jax.numpy as jnp
from jax import lax
from jax.experimental import pallas as pl
from jax.experimental.pallas import tpu as pltpu

D_MODEL = 1024
A_WIDTH = 512
B_WIDTH = 512
IN_WIDTH = 2 * A_WIDTH + 3 * B_WIDTH
A_GROUPS = 8
GROUP_DIM = 64
CHUNK = 128
HEADS = 8
HEAD_DIM = 64
PATTERNS = ((128, 1), (512, 4), (2048, 16))
HALF = 64
N_GROUPS = 4
EXPERTS_PER_GROUP = 8
N_EXPERTS = 32
D_EXPERT = 512
ROW_BLOCK = 256
EPS = 1e-6
NEG_INF = -1e30

LANES = 128
ROUTER_ROWS = 48
TM_PROJ = 256
TM_OUT = 512
TM_MOVE = 512
TQ = 128
VMEM_LIMIT = 48 * 1024 * 1024

F32 = jnp.float32
BF16 = jnp.bfloat16
NT_DIMS = (((1,), (1,)), ((), ()))


def _rms(x, gain):
    return x * lax.rsqrt(jnp.mean(x * x, axis=-1, keepdims=True) + EPS) * gain


def _proj_kernel(x_ref, gmix_ref, win_ref, bd_ref, avn_ref, wcat_ref, bias_ref, gq_ref, gk_ref, gna_ref,
                 ya_ref, q_ref, k_ref, v_ref):
    h = _rms(x_ref[...], gmix_ref[...])
    proj = jnp.dot(h.astype(BF16), win_ref[...], preferred_element_type=F32)
    pu = proj[:, 0:A_WIDTH]
    pv = proj[:, A_WIDTH:2 * A_WIDTH]
    q = proj[:, 2 * A_WIDTH:2 * A_WIDTH + B_WIDTH]
    k = proj[:, 2 * A_WIDTH + B_WIDTH:2 * A_WIDTH + 2 * B_WIDTH]
    v = proj[:, 2 * A_WIDTH + 2 * B_WIDTH:]
    bd = bd_ref[...]

    def group_norm(t, gain):
        ms = jnp.dot((t * t).astype(BF16), bd, preferred_element_type=F32) * (1.0 / GROUP_DIM)
        return t * lax.rsqrt(ms + EPS) * gain

    u = jax.nn.gelu(pu)
    vn = group_norm(jax.nn.gelu(pv), avn_ref[...]).astype(BF16)
    lane = lax.broadcasted_iota(jnp.int32, (CHUNK, LANES), 1)
    lo = lane < GROUP_DIM
    zero = jnp.zeros((CHUNK, LANES), BF16)
    chunks = []
    for c in range(x_ref.shape[0] // CHUNK):
        blks = []
        for j in range(A_WIDTH // LANES):
            vb = vn[c * CHUNK:(c + 1) * CHUNK, j * LANES:(j + 1) * LANES]
            rhs = jnp.concatenate([jnp.where(lo, vb, zero), jnp.where(lo, zero, vb)], axis=0)
            blks.append(jnp.dot(wcat_ref[j], rhs, preferred_element_type=F32))
        chunks.append(jnp.concatenate(blks, axis=1) + bias_ref[...])
    mixed = jnp.concatenate(chunks, axis=0)
    ya_ref[...] = _rms(u * mixed, gna_ref[...]).astype(BF16)
    q_ref[...] = group_norm(q, gq_ref[...]).astype(BF16)
    k_ref[...] = group_norm(k, gk_ref[...]).astype(BF16)
    v_ref[...] = v.astype(BF16)


def _proj_call(x2d, gmix, win, bd, avn, wcat, bias, gq, gk, gna):
    n = x2d.shape[0]
    tm = TM_PROJ
    full = lambda shape: pl.BlockSpec(shape, lambda i: (0,) * len(shape))
    tok = lambda w: pl.BlockSpec((tm, w), lambda i: (i, 0))
    return pl.pallas_call(
        _proj_kernel,
        grid=(n // tm,),
        in_specs=[tok(D_MODEL), full((1, D_MODEL)), full((D_MODEL, IN_WIDTH)), full((A_WIDTH, A_WIDTH)),
                  full((1, A_WIDTH)), full((A_WIDTH // LANES, CHUNK, 2 * CHUNK)), full((CHUNK, A_WIDTH)),
                  full((1, B_WIDTH)), full((1, B_WIDTH)), full((1, A_WIDTH))],
        out_specs=[tok(A_WIDTH), tok(B_WIDTH), tok(B_WIDTH), tok(B_WIDTH)],
        out_shape=[jax.ShapeDtypeStruct((n, A_WIDTH), BF16)] + [jax.ShapeDtypeStruct((n, B_WIDTH), BF16)] * 3,
        compiler_params=pltpu.CompilerParams(dimension_semantics=("parallel",), vmem_limit_bytes=VMEM_LIMIT),
        name="proj_gating",
    )(x2d, gmix, win, bd, avn, wcat, bias, gq, gk, gna)


def _attn_kernel(*refs, dil, n_sub, has_prev, last):
    q_ref, kp_ref, kc_ref, kn_ref, vp_ref, vc_ref, vn_ref = refs[:7]
    pos = 7
    if has_prev:
        acc_in_ref, ml_in_ref = refs[pos:pos + 2]
        pos += 2
    if last:
        gain_ref, yb_ref = refs[pos:pos + 2]
    else:
        acc_out_ref, ml_out_ref = refs[pos:pos + 2]

    i = pl.program_id(2)
    nk = TQ + 2 * HALF
    q = q_ref[...]
    kcat = jnp.concatenate([kp_ref[HALF:, :], kc_ref[...], kn_ref[:HALF, :]], axis=0)
    vcat = jnp.concatenate([vp_ref[HALF:, :], vc_ref[...], vn_ref[:HALF, :]], axis=0)
    row = lax.broadcasted_iota(jnp.int32, (TQ, nk), 0)
    col = lax.broadcasted_iota(jnp.int32, (TQ, nk), 1)
    dist = jnp.abs(col - HALF - row)
    kpos = i * TQ - HALF + col
    valid = (dist <= HALF) & (kpos >= 0) & (kpos < n_sub)
    distf = dist.astype(F32) * float(dil)
    lane = lax.broadcasted_iota(jnp.int32, (TQ, LANES), 1)
    lo = lane < HEAD_DIM
    if has_prev:
        ml_prev = ml_in_ref[...]
    ml_new = jnp.zeros((TQ, LANES), F32)
    blocks = []
    for j in range(B_WIDTH // LANES):
        qb = q[:, j * LANES:(j + 1) * LANES]
        kb = kcat[:, j * LANES:(j + 1) * LANES]
        vb = vcat[:, j * LANES:(j + 1) * LANES]
        res = []
        for hh in range(2):
            h = 2 * j + hh
            slope = 2.0 ** (-8.0 * (h + 1) / HEADS)
            qm = jnp.where(lo if hh == 0 else ~lo, qb, jnp.zeros_like(qb))
            s = lax.dot_general(qm, kb, NT_DIMS, preferred_element_type=F32)
            s = jnp.where(valid, s - slope * distf, NEG_INF)
            m_new = jnp.max(s, axis=-1, keepdims=True)
            if has_prev:
                m_prev = ml_prev[:, 16 * h:16 * h + 1]
                l_prev = ml_prev[:, 16 * h + 8:16 * h + 9]
                m_new = jnp.maximum(m_prev, m_new)
                alpha = jnp.exp(m_prev - m_new)
            p = jnp.exp(s - m_new)
            l_new = jnp.sum(p, axis=-1, keepdims=True)
            pv = jnp.dot(p.astype(BF16), vb, preferred_element_type=F32)
            if has_prev:
                l_new = alpha * l_prev + l_new
                pv = alpha * acc_in_ref[:, j * LANES:(j + 1) * LANES] + pv
            if last:
                pv = pv / l_new
            else:
                sel = (lane >> 4) == h
                ml_new = jnp.where(sel, jnp.where((lane & 8) == 0, m_new, l_new), ml_new)
            res.append(pv)
        blocks.append(jnp.where(lo, res[0], res[1]))
    out = jnp.concatenate(blocks, axis=1)
    if last:
        yb_ref[...] = _rms(out, gain_ref[...]).astype(BF16)
    else:
        acc_out_ref[...] = out
        ml_out_ref[...] = ml_new


def _attn_branch(q, k, v, prev, dil, gain_b):
    b, t, _ = q.shape
    n_sub = t // dil
    nblk = n_sub // TQ
    has_prev = prev is not None
    last = gain_b is not None
    view = lambda a: a.reshape(b, n_sub, dil * a.shape[-1])
    spec = lambda w, f: pl.BlockSpec((None, TQ, w), f)
    cur = lambda bb, r, i: (bb, i, r)
    prv = lambda bb, r, i: (bb, jnp.maximum(i - 1, 0), r)
    nxt = lambda bb, r, i: (bb, jnp.minimum(i + 1, nblk - 1), r)
    args = [view(q), view(k), view(k), view(k), view(v), view(v), view(v)]
    in_specs = [spec(B_WIDTH, cur), spec(B_WIDTH, prv), spec(B_WIDTH, cur), spec(B_WIDTH, nxt),
                spec(B_WIDTH, prv), spec(B_WIDTH, cur), spec(B_WIDTH, nxt)]
    if has_prev:
        args += [view(prev[0]), view(prev[1])]
        in_specs += [spec(B_WIDTH, cur), spec(LANES, cur)]
    if last:
        args.append(gain_b)
        in_specs.append(pl.BlockSpec((1, B_WIDTH), lambda bb, r, i: (0, 0)))
        out_specs = [spec(B_WIDTH, cur)]
        out_shape = [jax.ShapeDtypeStruct((b, n_sub, dil * B_WIDTH), BF16)]
    else:
        out_specs = [spec(B_WIDTH, cur), spec(LANES, cur)]
        out_shape = [jax.ShapeDtypeStruct((b, n_sub, dil * B_WIDTH), F32),
                     jax.ShapeDtypeStruct((b, n_sub, dil * LANES), F32)]
    outs = pl.pallas_call(
        functools.partial(_attn_kernel, dil=dil, n_sub=n_sub, has_prev=has_prev, last=last),
        grid=(b, dil, nblk),
        in_specs=in_specs,
        out_specs=out_specs,
        out_shape=out_shape,
        compiler_params=pltpu.CompilerParams(dimension_semantics=("parallel", "parallel", "parallel"),
                                             vmem_limit_bytes=VMEM_LIMIT),
        name=f"attn_dil{dil}",
    )(*args)
    return [o.reshape(b, t, o.shape[-1] // dil) for o in outs]


def _out_kernel(x_ref, ya_ref, yb_ref, wout_ref, gffn_ref, wr_ref, br_ref, tri_ref,
                x2_ref, xn_ref, ei_ref, gc_ref, cnt_ref, base_ref):
    tm = x_ref.shape[0]

    @pl.when(pl.program_id(0) == 0)
    def _():
        base_ref[...] = jnp.zeros_like(base_ref)

    a = jnp.concatenate([ya_ref[...], yb_ref[...]], axis=1)
    x2 = x_ref[...] + jnp.dot(a, wout_ref[...], preferred_element_type=F32)
    x2_ref[...] = x2
    xn = _rms(x2, gffn_ref[...])
    xn_ref[...] = xn

    lg = lax.dot_general(wr_ref[...], xn.astype(BF16), NT_DIMS, preferred_element_type=F32) + br_ref[...]
    e_log = lg[0:N_EXPERTS]
    g_log = lg[N_EXPERTS:N_EXPERTS + N_GROUPS]
    r4 = lax.broadcasted_iota(jnp.int32, (N_GROUPS, tm), 0).astype(F32)
    g_max = jnp.max(g_log, axis=0, keepdims=True)
    g_sel = jnp.min(jnp.where(g_log == g_max, r4, float(N_GROUPS)), axis=0, keepdims=True)
    p_group = 1.0 / jnp.sum(jnp.exp(g_log - g_max), axis=0, keepdims=True)
    e_sel = jnp.zeros((EXPERTS_PER_GROUP, tm), F32)
    for g in range(N_GROUPS):
        e_sel = jnp.where(g_sel == float(g), e_log[g * EXPERTS_PER_GROUP:(g + 1) * EXPERTS_PER_GROUP], e_sel)
    r8 = lax.broadcasted_iota(jnp.int32, (EXPERTS_PER_GROUP, tm), 0).astype(F32)
    v1 = jnp.max(e_sel, axis=0, keepdims=True)
    i1 = jnp.min(jnp.where(e_sel == v1, r8, float(EXPERTS_PER_GROUP)), axis=0, keepdims=True)
    e_rest = jnp.where(r8 == i1, -jnp.inf, e_sel)
    v2 = jnp.max(e_rest, axis=0, keepdims=True)
    i2 = jnp.min(jnp.where(e_rest == v2, r8, float(EXPERTS_PER_GROUP)), axis=0, keepdims=True)
    d = jnp.exp(v2 - v1)
    gate1 = p_group * (1.0 / (1.0 + d))
    gate2 = p_group * (d / (1.0 + d))
    eid1 = g_sel * float(EXPERTS_PER_GROUP) + i1
    eid2 = g_sel * float(EXPERTS_PER_GROUP) + i2

    r32 = lax.broadcasted_iota(jnp.int32, (N_EXPERTS, tm), 0).astype(F32)
    oh1 = r32 == eid1
    oh2 = r32 == eid2
    oh1f = jnp.where(oh1, 1.0, 0.0)
    oh2f = jnp.where(oh2, 1.0, 0.0)
    tri = tri_ref[...]
    pre1 = jnp.dot(oh1f.astype(BF16), tri, preferred_element_type=F32)
    pre2 = jnp.dot(oh2f.astype(BF16), tri, preferred_element_type=F32)
    tot1 = jnp.sum(oh1f, axis=1, keepdims=True)
    tot2 = jnp.sum(oh2f, axis=1, keepdims=True)
    base_full = base_ref[...]
    base = base_full[:, 0:1]
    rank1 = jnp.sum(jnp.where(oh1, base + pre1, 0.0), axis=0, keepdims=True)
    rank2 = jnp.sum(jnp.where(oh2, base + tot1 + pre2, 0.0), axis=0, keepdims=True)
    base_full = base_full + tot1 + tot2
    base_ref[...] = base_full
    cnt_ref[...] = base_full
    ei_ref[...] = jnp.concatenate([eid1, eid2, rank1, rank2], axis=0).astype(jnp.int32)
    r128 = lax.broadcasted_iota(jnp.int32, (LANES, tm), 0)
    gates_rows = jnp.where(r128 == 0, gate1, jnp.where(r128 == 1, gate2, 0.0))
    gc_ref[...] = gates_rows.T


def _out_call(x2d, ya, yb, wout, gffn, wr, br, tri):
    n = x2d.shape[0]
    tm = TM_OUT
    full = lambda shape: pl.BlockSpec(shape, lambda i: (0,) * len(shape))
    tok = lambda w: pl.BlockSpec((tm, w), lambda i: (i, 0))
    return pl.pallas_call(
        _out_kernel,
        grid=(n // tm,),
        in_specs=[tok(D_MODEL), tok(A_WIDTH), tok(B_WIDTH), full((D_MODEL, D_MODEL)), full((1, D_MODEL)),
                  full((ROUTER_ROWS, D_MODEL)), full((ROUTER_ROWS, 1)), full((tm, tm))],
        out_specs=[tok(D_MODEL), tok(D_MODEL), pl.BlockSpec((4, tm), lambda i: (0, i)), tok(LANES),
                   full((N_EXPERTS, LANES))],
        out_shape=[jax.ShapeDtypeStruct((n, D_MODEL), F32), jax.ShapeDtypeStruct((n, D_MODEL), F32),
                   jax.ShapeDtypeStruct((4, n), jnp.int32), jax.ShapeDtypeStruct((n, LANES), F32),
                   jax.ShapeDtypeStruct((N_EXPERTS, LANES), F32)],
        scratch_shapes=[pltpu.VMEM((N_EXPERTS, LANES), F32)],
        compiler_params=pltpu.CompilerParams(dimension_semantics=("arbitrary",), vmem_limit_bytes=VMEM_LIMIT),
        name="out_router",
    )(x2d, ya, yb, wout, gffn, wr, br, tri)


def _row_copy(src, s, dst, d, sem):
    return pltpu.make_async_copy(src.at[pl.ds(s, 1), :], dst.at[pl.ds(d, 1), :], sem)


def _dispatch_kernel(d0_ref, d1_ref, xn_ref, rows_in_ref, rows_ref, sem):
    del rows_in_ref
    tm = xn_ref.shape[0]

    def issue(t, c):
        _row_copy(xn_ref, t, rows_ref, d0_ref[t], sem).start()
        _row_copy(xn_ref, t, rows_ref, d1_ref[t], sem).start()
        return c

    lax.fori_loop(0, tm, issue, 0)

    def drain(t, c):
        _row_copy(xn_ref, 0, rows_ref, 0, sem).wait()
        _row_copy(xn_ref, 0, rows_ref, 0, sem).wait()
        return c

    lax.fori_loop(0, tm, drain, 0)


def _dispatch_call(dest0, dest1, xn, n_rows):
    n = xn.shape[0]
    tm = TM_MOVE
    idx = pl.BlockSpec((tm,), lambda i: (i,), memory_space=pltpu.SMEM)
    return pl.pallas_call(
        _dispatch_kernel,
        grid=(n // tm,),
        in_specs=[idx, idx, pl.BlockSpec((tm, D_MODEL), lambda i: (i, 0)), pl.BlockSpec(memory_space=pl.ANY)],
        out_specs=pl.BlockSpec(memory_space=pl.ANY),
        out_shape=jax.ShapeDtypeStruct((n_rows, D_MODEL), F32),
        scratch_shapes=[pltpu.SemaphoreType.DMA],
        input_output_aliases={3: 0},
        compiler_params=pltpu.CompilerParams(dimension_semantics=("arbitrary",), has_side_effects=True),
        name="dispatch",
    )(dest0, dest1, xn, jnp.zeros((n_rows, D_MODEL), F32))


def _expert_kernel(be_ref, x_ref, wg_ref, wu_ref, wd_ref, y_ref):
    del be_ref
    xb = x_ref[...].astype(BF16)
    g = jnp.dot(xb, wg_ref[...], preferred_element_type=F32)
    u = jnp.dot(xb, wu_ref[...], preferred_element_type=F32)
    h = (jax.nn.silu(g) * u).astype(BF16)
    y_ref[...] = jnp.dot(h, wd_ref[...], preferred_element_type=F32)


def _expert_call(block_e, x_rows, wg, wu, wd):
    n_rows = x_rows.shape[0]
    nb = n_rows // ROW_BLOCK
    grid_spec = pltpu.PrefetchScalarGridSpec(
        num_scalar_prefetch=1,
        grid=(nb,),
        in_specs=[pl.BlockSpec((ROW_BLOCK, D_MODEL), lambda b, be: (b, 0)),
                  pl.BlockSpec((None, D_MODEL, D_EXPERT), lambda b, be: (be[b], 0, 0)),
                  pl.BlockSpec((None, D_MODEL, D_EXPERT), lambda b, be: (be[b], 0, 0)),
                  pl.BlockSpec((None, D_EXPERT, D_MODEL), lambda b, be: (be[b], 0, 0))],
        out_specs=pl.BlockSpec((ROW_BLOCK, D_MODEL), lambda b, be: (b, 0)),
    )
    return pl.pallas_call(
        _expert_kernel,
        grid_spec=grid_spec,
        out_shape=jax.ShapeDtypeStruct((n_rows, D_MODEL), F32),
        compiler_params=pltpu.CompilerParams(dimension_semantics=("arbitrary",), vmem_limit_bytes=VMEM_LIMIT),
        name="experts",
    )(block_e, x_rows, wg, wu, wd)


def _combine_kernel(d0_ref, d1_ref, x2_ref, gc_ref, y_ref, o_ref, y0_ref, y1_ref, sem):
    tm = x2_ref.shape[0]

    def issue(t, c):
        _row_copy(y_ref, d0_ref[t], y0_ref, t, sem).start()
        _row_copy(y_ref, d1_ref[t], y1_ref, t, sem).start()
        return c

    lax.fori_loop(0, tm, issue, 0)

    def drain(t, c):
        _row_copy(y_ref, 0, y0_ref, 0, sem).wait()
        _row_copy(y_ref, 0, y1_ref, 0, sem).wait()
        return c

    lax.fori_loop(0, tm, drain, 0)
    gc = gc_ref[...]
    o_ref[...] = x2_ref[...] + (gc[:, 0:1] * y0_ref[...] + gc[:, 1:2] * y1_ref[...])


def _combine_call(dest0, dest1, x2, gc, y_rows):
    n = x2.shape[0]
    tm = TM_MOVE
    idx = pl.BlockSpec((tm,), lambda i: (i,), memory_space=pltpu.SMEM)
    return pl.pallas_call(
        _combine_kernel,
        grid=(n // tm,),
        in_specs=[idx, idx, pl.BlockSpec((tm, D_MODEL), lambda i: (i, 0)), pl.BlockSpec((tm, LANES), lambda i: (i, 0)),
                  pl.BlockSpec(memory_space=pl.ANY)],
        out_specs=pl.BlockSpec((tm, D_MODEL), lambda i: (i, 0)),
        out_shape=jax.ShapeDtypeStruct((n, D_MODEL), F32),
        scratch_shapes=[pltpu.VMEM((tm, D_MODEL), F32), pltpu.VMEM((tm, D_MODEL), F32), pltpu.SemaphoreType.DMA],
        compiler_params=pltpu.CompilerParams(dimension_semantics=("arbitrary",), vmem_limit_bytes=VMEM_LIMIT),
        name="combine",
    )(dest0, dest1, x2, gc, y_rows)


def _prepare(norm_mix, w_in, a_v_norm, a_spatial_w, a_spatial_b, q_norm, k_norm, out_norm_a, out_norm_b, w_out,
             norm_ffn, w_router_group, b_router_group, w_router_expert, b_router_expert,
             w_expert_gate, w_expert_up, w_expert_down):
    ch = jnp.arange(A_WIDTH) // GROUP_DIM
    pad = ROUTER_ROWS - N_EXPERTS - N_GROUPS
    return dict(
        gmix=norm_mix.reshape(1, D_MODEL),
        win=w_in.astype(BF16),
        bd=(ch[:, None] == ch[None, :]).astype(BF16),
        avn=a_v_norm.reshape(1, A_WIDTH),
        wcat=jnp.concatenate([a_spatial_w[0::2], a_spatial_w[1::2]], axis=2).astype(BF16),
        bias=jnp.repeat(a_spatial_b.T, GROUP_DIM, axis=1),
        gq=(jnp.tile(q_norm, HEADS) * (HEAD_DIM ** -0.5)).reshape(1, B_WIDTH),
        gk=jnp.tile(k_norm, HEADS).reshape(1, B_WIDTH),
        gna=out_norm_a.reshape(1, A_WIDTH),
        gnb=out_norm_b.reshape(1, B_WIDTH),
        wout=w_out.astype(BF16),
        gffn=norm_ffn.reshape(1, D_MODEL),
        wr=jnp.concatenate([w_router_expert.T, w_router_group.T, jnp.zeros((pad, D_MODEL), F32)], axis=0).astype(BF16),
        br=jnp.concatenate([b_router_expert, b_router_group, jnp.zeros((pad,), F32)]).reshape(ROUTER_ROWS, 1),
        tri=(jnp.arange(TM_OUT)[:, None] < jnp.arange(TM_OUT)[None, :]).astype(BF16),
        wg=w_expert_gate.astype(BF16),
        wu=w_expert_up.astype(BF16),
        wd=w_expert_down.astype(BF16),
    )


def _layer(x, p):
    b, t, _ = x.shape
    n = b * t
    x2d = x.reshape(n, D_MODEL)
    ya, q, k, v = _proj_call(x2d, p["gmix"], p["win"], p["bd"], p["avn"], p["wcat"], p["bias"], p["gq"], p["gk"], p["gna"])
    q, k, v = (a.reshape(b, t, B_WIDTH) for a in (q, k, v))
    prev = None
    for idx, (_, dil) in enumerate(PATTERNS):
        is_last = idx == len(PATTERNS) - 1
        prev = _attn_branch(q, k, v, prev, dil, p["gnb"] if is_last else None)
    yb = prev[0].reshape(n, B_WIDTH)
    x2, xn, ei, gc, cnt = _out_call(x2d, ya, yb, p["wout"], p["gffn"], p["wr"], p["br"], p["tri"])

    counts = cnt[:, 0].astype(jnp.int32)
    pcounts = (counts + ROW_BLOCK - 1) // ROW_BLOCK * ROW_BLOCK
    pends = jnp.cumsum(pcounts)
    pstarts = pends - pcounts
    nb = (2 * n) // ROW_BLOCK + N_EXPERTS
    block_e = jnp.minimum(jnp.searchsorted(pends, jnp.arange(nb, dtype=jnp.int32) * ROW_BLOCK, side="right"),
                          N_EXPERTS - 1).astype(jnp.int32)
    dest0 = pstarts[ei[0]] + ei[2]
    dest1 = pstarts[ei[1]] + ei[3]

    x_rows = _dispatch_call(dest0, dest1, xn, nb * ROW_BLOCK)
    y_rows = _expert_call(block_e, x_rows, p["wg"], p["wu"], p["wd"])
    out = _combine_call(dest0, dest1, x2, gc, y_rows)
    return out.reshape(b, t, D_MODEL)


def kernel(x_prompt, x_sample, norm_mix, w_in, a_v_norm, a_spatial_w, a_spatial_b, q_norm, k_norm, out_norm_a,
           out_norm_b, w_out, norm_ffn, w_router_group, b_router_group, w_router_expert, b_router_expert,
           w_expert_gate, w_expert_up, w_expert_down):
    depth = norm_mix.shape[0]
    layers = [
        _prepare(norm_mix[l], w_in[l], a_v_norm[l], a_spatial_w[l], a_spatial_b[l], q_norm[l], k_norm[l],
                 out_norm_a[l], out_norm_b[l], w_out[l], norm_ffn[l], w_router_group[l], b_router_group[l],
                 w_router_expert[l], b_router_expert[l], w_expert_gate[l], w_expert_up[l], w_expert_down[l])
        for l in range(depth)
    ]

    def run(x):
        for p in layers:
            x = _layer(x, p)
        return x

    return (run(x_prompt), run(x_sample))
```

```python
import functools

import jax
import jax.numpy as jnp
from jax import lax
from jax.experimental import pallas as pl
from jax.experimental.pallas import tpu as pltpu

D_MODEL = 1024
A_WIDTH = 512
B_WIDTH = 512
IN_WIDTH = 2 * A_WIDTH + 3 * B_WIDTH
A_GROUPS = 8
GROUP_DIM = 64
CHUNK = 128
HEADS = 8
HEAD_DIM = 64
DILATIONS = (1, 4, 16)
HALF = 64
N_GROUPS = 4
EXPERTS_PER_GROUP = 8
N_EXPERTS = 32
D_EXPERT = 512
ROW_BLOCK = 256
EPS = 1e-6
NEG_INF = -1e30

LANES = 128
ROUTER_ROWS = 48
TM_PROJ = 256
TM_OUT = 512
TM_MOVE = 512
MOVE_UNROLL = 8
TQ = 128
UNITS_PER_ITER = 4
SUPER = TQ * max(DILATIONS)
HALO = HALF * max(DILATIONS)
VMEM_LIMIT = 48 * 1024 * 1024

F32 = jnp.float32
BF16 = jnp.bfloat16
NT_DIMS = (((1,), (1,)), ((), ()))


def _rms(x, gain):
    return x * lax.rsqrt(jnp.mean(x * x, axis=-1, keepdims=True) + EPS) * gain


def _proj_kernel(x_ref, gmix_ref, win_ref, bd_ref, avn_ref, wcat_ref, bias_ref, gq_ref, gk_ref, gna_ref,
                 ya_ref, q_ref, k_ref, v_ref):
    h = _rms(x_ref[...], gmix_ref[...])
    proj = jnp.dot(h.astype(BF16), win_ref[...], preferred_element_type=F32)
    pu = proj[:, 0:A_WIDTH]
    pv = proj[:, A_WIDTH:2 * A_WIDTH]
    q = proj[:, 2 * A_WIDTH:2 * A_WIDTH + B_WIDTH]
    k = proj[:, 2 * A_WIDTH + B_WIDTH:2 * A_WIDTH + 2 * B_WIDTH]
    v = proj[:, 2 * A_WIDTH + 2 * B_WIDTH:]
    bd = bd_ref[...]

    def group_norm(t, gain):
        ms = jnp.dot((t * t).astype(BF16), bd, preferred_element_type=F32) * (1.0 / GROUP_DIM)
        return t * lax.rsqrt(ms + EPS) * gain

    u = jax.nn.gelu(pu)
    vn = group_norm(jax.nn.gelu(pv), avn_ref[...]).astype(BF16)
    lane = lax.broadcasted_iota(jnp.int32, (CHUNK, LANES), 1)
    lo = lane < GROUP_DIM
    zero = jnp.zeros((CHUNK, LANES), BF16)
    chunks = []
    for c in range(x_ref.shape[0] // CHUNK):
        blks = []
        for j in range(A_WIDTH // LANES):
            vb = vn[c * CHUNK:(c + 1) * CHUNK, j * LANES:(j + 1) * LANES]
            rhs = jnp.concatenate([jnp.where(lo, vb, zero), jnp.where(lo, zero, vb)], axis=0)
            blks.append(jnp.dot(wcat_ref[j], rhs, preferred_element_type=F32))
        chunks.append(jnp.concatenate(blks, axis=1) + bias_ref[...])
    mixed = jnp.concatenate(chunks, axis=0)
    ya_ref[...] = _rms(u * mixed, gna_ref[...]).astype(BF16)
    q_ref[...] = group_norm(q, gq_ref[...]).astype(BF16)
    k_ref[...] = group_norm(k, gk_ref[...]).astype(BF16)
    v_ref[...] = v.astype(BF16)


def _proj_call(x2d, gmix, win, bd, avn, wcat, bias, gq, gk, gna):
    n = x2d.shape[0]
    tm = TM_PROJ
    full = lambda shape: pl.BlockSpec(shape, lambda i: (0,) * len(shape))
    tok = lambda w: pl.BlockSpec((tm, w), lambda i: (i, 0))
    return pl.pallas_call(
        _proj_kernel,
        grid=(n // tm,),
        in_specs=[tok(D_MODEL), full((1, D_MODEL)), full((D_MODEL, IN_WIDTH)), full((A_WIDTH, A_WIDTH)),
                  full((1, A_WIDTH)), full((A_WIDTH // LANES, CHUNK, 2 * CHUNK)), full((CHUNK, A_WIDTH)),
                  full((1, B_WIDTH)), full((1, B_WIDTH)), full((1, A_WIDTH))],
        out_specs=[tok(A_WIDTH), tok(B_WIDTH), tok(B_WIDTH), tok(B_WIDTH)],
        out_shape=[jax.ShapeDtypeStruct((n, A_WIDTH), BF16)] + [jax.ShapeDtypeStruct((n, B_WIDTH), BF16)] * 3,
        compiler_params=pltpu.CompilerParams(dimension_semantics=("parallel",), vmem_limit_bytes=VMEM_LIMIT),
        name="proj_gating",
    )(x2d, gmix, win, bd, avn, wcat, bias, gq, gk, gna)


def _attn_kernel(*refs, seq_len, halo):
    if halo:
        slope_ref, q_ref, kp_ref, kc_ref, kn_ref, vp_ref, vc_ref, vn_ref, o_ref = refs[:9]
        scratch = refs[9:]
    else:
        slope_ref, q_ref, kc_ref, vc_ref, o_ref = refs[:5]
        scratch = refs[5:]
    qf, kf, vf, acc16, m16, l16, acc4, m4, l4 = scratch
    sb = pl.program_id(1)

    qf[...] = q_ref[...].astype(F32)
    if halo:
        kf[0:halo, :] = kp_ref[...].astype(F32)
        kf[halo + SUPER:, :] = kn_ref[...].astype(F32)
        vf[0:halo, :] = vp_ref[...].astype(F32)
        vf[halo + SUPER:, :] = vn_ref[...].astype(F32)
    kf[halo:halo + SUPER, :] = kc_ref[...].astype(F32)
    vf[halo:halo + SUPER, :] = vc_ref[...].astype(F32)

    lane = lax.broadcasted_iota(jnp.int32, (TQ, LANES), 1)
    lo = lane < HEAD_DIM

    def branch_unit(dil, res, qs):
        n_sub = SUPER // dil
        if halo:
            tk = TQ + 2 * HALF
            ks = qs - HALF
        else:
            tk = min(TQ + 2 * HALF, n_sub)
            ks = jnp.clip(qs - HALF, 0, n_sub - tk)
        q_start = dil * qs + res
        k_start = halo + dil * ks + res
        if dil == 1:
            q_start = pl.multiple_of(q_start, HALF)
            k_start = pl.multiple_of(k_start, HALF)
            qb = qf[pl.ds(q_start, TQ), :]
            kb = kf[pl.ds(k_start, tk), :]
            vb = vf[pl.ds(k_start, tk), :]
        else:
            qb = qf[pl.ds(q_start, TQ, stride=dil), :]
            kb = kf[pl.ds(k_start, tk, stride=dil), :]
            vb = vf[pl.ds(k_start, tk, stride=dil), :]
        qb = qb.astype(BF16)
        kb = kb.astype(BF16)
        vb = vb.astype(BF16)
        row = lax.broadcasted_iota(jnp.int32, (TQ, tk), 0)
        col = lax.broadcasted_iota(jnp.int32, (TQ, tk), 1)
        dist = jnp.abs(col - row + (ks - qs))
        valid = dist <= HALF
        if halo:
            tok = sb * SUPER + dil * (ks + col) + res
            valid = valid & (tok >= 0) & (tok < seq_len)
        distf = dist.astype(F32) * float(dil)
        res_h = []
        for hh in range(2):
            qm = jnp.where(lo if hh == 0 else ~lo, qb, jnp.zeros_like(qb))
            s = lax.dot_general(qm, kb, NT_DIMS, preferred_element_type=F32)
            s = jnp.where(valid, s - slope_ref[hh:hh + 1, 0:tk] * distf, NEG_INF)
            m = jnp.max(s, axis=-1, keepdims=True)
            p = jnp.exp(s - m)
            l = jnp.sum(p, axis=-1, keepdims=True)
            pv = jnp.dot(p.astype(BF16), vb, preferred_element_type=F32)
            res_h.append((pv, m, l))
        return tuple(jnp.where(lo, a, b) for a, b in zip(res_h[0], res_h[1]))

    def strided_branch(dil, acc_ref, m_ref, l_ref):
        def body(g, c):
            for i in range(UNITS_PER_ITER):
                u = g * UNITS_PER_ITER + i
                res = u % dil
                qs = (u // dil) * TQ
                acc, m, l = branch_unit(dil, res, qs)
                rows = pl.ds(dil * qs + res, TQ, stride=dil)
                acc_ref[rows, :] = acc
                m_ref[rows, :] = m
                l_ref[rows, :] = l
            return c

        lax.fori_loop(0, SUPER // TQ // UNITS_PER_ITER, body, 0)

    strided_branch(16, acc16, m16, l16)
    strided_branch(4, acc4, m4, l4)

    def merge(g, c):
        for i in range(UNITS_PER_ITER):
            qs = (g * UNITS_PER_ITER + i) * TQ
            a1, m1, l1 = branch_unit(1, 0, qs)
            rows = pl.ds(pl.multiple_of(qs, TQ), TQ)
            m_4, m_16 = m4[rows, :], m16[rows, :]
            m_all = jnp.maximum(jnp.maximum(m1, m_4), m_16)
            e1 = jnp.exp(m1 - m_all)
            e4 = jnp.exp(m_4 - m_all)
            e16 = jnp.exp(m_16 - m_all)
            num = e1 * a1 + e4 * acc4[rows, :] + e16 * acc16[rows, :]
            den = e1 * l1 + e4 * l4[rows, :] + e16 * l16[rows, :]
            o_ref[rows, :] = num / den
        return c

    lax.fori_loop(0, SUPER // TQ // UNITS_PER_ITER, merge, 0)


def _attn_call(q, k, v, slopes):
    b, t, _ = q.shape
    nsb = t // SUPER
    halo = HALO if nsb > 1 else 0
    nkb = t // HALO
    per = SUPER // HALO
    cur = lambda bb, s, j: (bb, s, j)
    blk = lambda rows, f: pl.BlockSpec((None, rows, LANES), f)
    slope_spec = pl.BlockSpec((None, 8, TQ + 2 * HALF), lambda bb, s, j: (j, 0, 0))
    if halo:
        prv = lambda bb, s, j: (bb, jnp.maximum(s * per - 1, 0), j)
        nxt = lambda bb, s, j: (bb, jnp.minimum((s + 1) * per, nkb - 1), j)
        args = [slopes, q, k, k, k, v, v, v]
        in_specs = [slope_spec, blk(SUPER, cur), blk(HALO, prv), blk(SUPER, cur), blk(HALO, nxt),
                    blk(HALO, prv), blk(SUPER, cur), blk(HALO, nxt)]
    else:
        args = [slopes, q, k, v]
        in_specs = [slope_spec, blk(SUPER, cur), blk(SUPER, cur), blk(SUPER, cur)]
    stat = pltpu.VMEM((SUPER, LANES), F32)
    kv = pltpu.VMEM((SUPER + 2 * halo, LANES), F32)
    return pl.pallas_call(
        functools.partial(_attn_kernel, seq_len=t, halo=halo),
        grid=(b, nsb, B_WIDTH // LANES),
        in_specs=in_specs,
        out_specs=blk(SUPER, cur),
        out_shape=jax.ShapeDtypeStruct((b, t, B_WIDTH), F32),
        scratch_shapes=[stat, kv, kv, stat, stat, stat, stat, stat, stat],
        compiler_params=pltpu.CompilerParams(dimension_semantics=("parallel", "parallel", "parallel"),
                                             vmem_limit_bytes=VMEM_LIMIT),
        name="attention",
    )(*args)


def _out_kernel(x_ref, ya_ref, att_ref, gnb_ref, wout_ref, gffn_ref, wr_ref, br_ref, tri_ref,
                x2_ref, xn_ref, ei_ref, gc_ref, cnt_ref, base_ref):
    tm = x_ref.shape[0]

    @pl.when(pl.program_id(0) == 0)
    def _():
        base_ref[...] = jnp.zeros_like(base_ref)

    yb = _rms(att_ref[...], gnb_ref[...]).astype(BF16)
    a = jnp.concatenate([ya_ref[...], yb], axis=1)
    x2 = x_ref[...] + jnp.dot(a, wout_ref[...], preferred_element_type=F32)
    x2_ref[...] = x2
    xn = _rms(x2, gffn_ref[...])
    xn_ref[...] = xn

    lg = lax.dot_general(wr_ref[...], xn.astype(BF16), NT_DIMS, preferred_element_type=F32) + br_ref[...]
    e_log = lg[0:N_EXPERTS]
    g_log = lg[N_EXPERTS:N_EXPERTS + N_GROUPS]
    r4 = lax.broadcasted_iota(jnp.int32, (N_GROUPS, tm), 0).astype(F32)
    g_max = jnp.max(g_log, axis=0, keepdims=True)
    g_sel = jnp.min(jnp.where(g_log == g_max, r4, float(N_GROUPS)), axis=0, keepdims=True)
    p_group = 1.0 / jnp.sum(jnp.exp(g_log - g_max), axis=0, keepdims=True)
    e_sel = jnp.zeros((EXPERTS_PER_GROUP, tm), F32)
    for g in range(N_GROUPS):
        e_sel = jnp.where(g_sel == float(g), e_log[g * EXPERTS_PER_GROUP:(g + 1) * EXPERTS_PER_GROUP], e_sel)
    r8 = lax.broadcasted_iota(jnp.int32, (EXPERTS_PER_GROUP, tm), 0).astype(F32)
    v1 = jnp.max(e_sel, axis=0, keepdims=True)
    i1 = jnp.min(jnp.where(e_sel == v1, r8, float(EXPERTS_PER_GROUP)), axis=0, keepdims=True)
    e_rest = jnp.where(r8 == i1, -jnp.inf, e_sel)
    v2 = jnp.max(e_rest, axis=0, keepdims=True)
    i2 = jnp.min(jnp.where(e_rest == v2, r8, float(EXPERTS_PER_GROUP)), axis=0, keepdims=True)
    d = jnp.exp(v2 - v1)
    gate1 = p_group * (1.0 / (1.0 + d))
    gate2 = p_group * (d / (1.0 + d))
    eid1 = g_sel * float(EXPERTS_PER_GROUP) + i1
    eid2 = g_sel * float(EXPERTS_PER_GROUP) + i2

    r32 = lax.broadcasted_iota(jnp.int32, (N_EXPERTS, tm), 0).astype(F32)
    oh1 = r32 == eid1
    oh2 = r32 == eid2
    oh1f = jnp.where(oh1, 1.0, 0.0)
    oh2f = jnp.where(oh2, 1.0, 0.0)
    tri = tri_ref[...]
    pre1 = jnp.dot(oh1f.astype(BF16), tri, preferred_element_type=F32)
    pre2 = jnp.dot(oh2f.astype(BF16), tri, preferred_element_type=F32)
    tot1 = jnp.sum(oh1f, axis=1, keepdims=True)
    tot2 = jnp.sum(oh2f, axis=1, keepdims=True)
    base_full = base_ref[...]
    base = base_full[:, 0:1]
    rank1 = jnp.sum(jnp.where(oh1, base + pre1, 0.0), axis=0, keepdims=True)
    rank2 = jnp.sum(jnp.where(oh2, base + tot1 + pre2, 0.0), axis=0, keepdims=True)
    base_full = base_full + tot1 + tot2
    base_ref[...] = base_full
    cnt_ref[...] = base_full
    ei_ref[...] = jnp.concatenate([eid1, eid2, rank1, rank2], axis=0).astype(jnp.int32)
    r128 = lax.broadcasted_iota(jnp.int32, (LANES, tm), 0)
    gates_rows = jnp.where(r128 == 0, gate1, jnp.where(r128 == 1, gate2, 0.0))
    gc_ref[...] = gates_rows.T


def _out_call(x2d, ya, att, gnb, wout, gffn, wr, br, tri):
    n = x2d.shape[0]
    tm = TM_OUT
    full = lambda shape: pl.BlockSpec(shape, lambda i: (0,) * len(shape))
    tok = lambda w: pl.BlockSpec((tm, w), lambda i: (i, 0))
    return pl.pallas_call(
        _out_kernel,
        grid=(n // tm,),
        in_specs=[tok(D_MODEL), tok(A_WIDTH), tok(B_WIDTH), full((1, B_WIDTH)), full((D_MODEL, D_MODEL)),
                  full((1, D_MODEL)), full((ROUTER_ROWS, D_MODEL)), full((ROUTER_ROWS, 1)), full((tm, tm))],
        out_specs=[tok(D_MODEL), tok(D_MODEL), pl.BlockSpec((4, tm), lambda i: (0, i)), tok(LANES),
                   full((N_EXPERTS, LANES))],
        out_shape=[jax.ShapeDtypeStruct((n, D_MODEL), F32), jax.ShapeDtypeStruct((n, D_MODEL), F32),
                   jax.ShapeDtypeStruct((4, n), jnp.int32), jax.ShapeDtypeStruct((n, LANES), F32),
                   jax.ShapeDtypeStruct((N_EXPERTS, LANES), F32)],
        scratch_shapes=[pltpu.VMEM((N_EXPERTS, LANES), F32)],
        compiler_params=pltpu.CompilerParams(dimension_semantics=("arbitrary",), vmem_limit_bytes=VMEM_LIMIT),
        name="out_router",
    )(x2d, ya, att, gnb, wout, gffn, wr, br, tri)


def _row_copy(src, s, dst, d, sem):
    return pltpu.make_async_copy(src.at[pl.ds(s, 1), :], dst.at[pl.ds(d, 1), :], sem)


def _dispatch_kernel(zb_ref, zon_ref, d0_ref, d1_ref, xn_ref, rows_ref, zero_ref, sem, zsem):
    tm = xn_ref.shape[0]

    @pl.when(pl.program_id(0) == 0)
    def _():
        zero_ref[...] = jnp.zeros_like(zero_ref)

        def zero_copy(i):
            start = pl.multiple_of(zb_ref[i] * ROW_BLOCK, ROW_BLOCK)
            return pltpu.make_async_copy(zero_ref, rows_ref.at[pl.ds(start, ROW_BLOCK), :], zsem)

        def start(i, c):
            @pl.when(zon_ref[i] == 1)
            def _():
                zero_copy(i).start()
            return c

        def wait(i, c):
            @pl.when(zon_ref[i] == 1)
            def _():
                zero_copy(i).wait()
            return c

        lax.fori_loop(0, 2 * N_EXPERTS, start, 0)
        lax.fori_loop(0, 2 * N_EXPERTS, wait, 0)

    def issue(g, c):
        for i in range(MOVE_UNROLL):
            t = g * MOVE_UNROLL + i
            _row_copy(xn_ref, t, rows_ref, d0_ref[t], sem).start()
            _row_copy(xn_ref, t, rows_ref, d1_ref[t], sem).start()
        return c

    lax.fori_loop(0, tm // MOVE_UNROLL, issue, 0)
    tile = pltpu.make_async_copy(xn_ref, rows_ref.at[pl.ds(0, tm), :], sem)
    tile.wait()
    tile.wait()


def _dispatch_call(zero_blocks, zero_on, dest0, dest1, xn, n_rows):
    n = xn.shape[0]
    tm = TM_MOVE
    idx = pl.BlockSpec((tm,), lambda i, zb, zon: (i,), memory_space=pltpu.SMEM)
    grid_spec = pltpu.PrefetchScalarGridSpec(
        num_scalar_prefetch=2,
        grid=(n // tm,),
        in_specs=[idx, idx, pl.BlockSpec((tm, D_MODEL), lambda i, zb, zon: (i, 0))],
        out_specs=pl.BlockSpec(memory_space=pl.ANY),
        scratch_shapes=[pltpu.VMEM((ROW_BLOCK, D_MODEL), F32), pltpu.SemaphoreType.DMA, pltpu.SemaphoreType.DMA],
    )
    return pl.pallas_call(
        _dispatch_kernel,
        grid_spec=grid_spec,
        out_shape=jax.ShapeDtypeStruct((n_rows, D_MODEL), F32),
        compiler_params=pltpu.CompilerParams(dimension_semantics=("arbitrary",), has_side_effects=True,
                                             disable_bounds_checks=True),
        name="dispatch",
    )(zero_blocks, zero_on, dest0, dest1, xn)


def _expert_kernel(be_ref, bv_ref, x_ref, wg_ref, wu_ref, wd_ref, y_ref):
    del be_ref
    valid = bv_ref[pl.program_id(0)]

    @pl.when(valid > 0)
    def _():
        xb = x_ref[...].astype(BF16)
        g = jnp.dot(xb, wg_ref[...], preferred_element_type=F32)
        u = jnp.dot(xb, wu_ref[...], preferred_element_type=F32)
        h = (jax.nn.silu(g) * u).astype(BF16)
        y_ref[...] = jnp.dot(h, wd_ref[...], preferred_element_type=F32)

    @pl.when(valid == 0)
    def _():
        y_ref[...] = jnp.zeros_like(y_ref)


def _expert_call(block_e, block_valid, x_rows, wg, wu, wd):
    n_rows = x_rows.shape[0]
    nb = n_rows // ROW_BLOCK
    grid_spec = pltpu.PrefetchScalarGridSpec(
        num_scalar_prefetch=2,
        grid=(nb,),
        in_specs=[pl.BlockSpec((ROW_BLOCK, D_MODEL), lambda b, be, bv: (b, 0)),
                  pl.BlockSpec((None, D_MODEL, D_EXPERT), lambda b, be, bv: (be[b], 0, 0)),
                  pl.BlockSpec((None, D_MODEL, D_EXPERT), lambda b, be, bv: (be[b], 0, 0)),
                  pl.BlockSpec((None, D_EXPERT, D_MODEL), lambda b, be, bv: (be[b], 0, 0))],
        out_specs=pl.BlockSpec((ROW_BLOCK, D_MODEL), lambda b, be, bv: (b, 0)),
    )
    return pl.pallas_call(
        _expert_kernel,
        grid_spec=grid_spec,
        out_shape=jax.ShapeDtypeStruct((n_rows, D_MODEL), F32),
        compiler_params=pltpu.CompilerParams(dimension_semantics=("arbitrary",), vmem_limit_bytes=VMEM_LIMIT),
        name="experts",
    )(block_e, block_valid, x_rows, wg, wu, wd)


def _combine_kernel(d0_ref, d1_ref, x2_ref, gc_ref, y_ref, o_ref, y0_ref, y1_ref, sem):
    tm = x2_ref.shape[0]

    def issue(g, c):
        for i in range(MOVE_UNROLL):
            t = g * MOVE_UNROLL + i
            _row_copy(y_ref, d0_ref[t], y0_ref, t, sem).start()
            _row_copy(y_ref, d1_ref[t], y1_ref, t, sem).start()
        return c

    lax.fori_loop(0, tm // MOVE_UNROLL, issue, 0)
    pltpu.make_async_copy(y_ref.at[pl.ds(0, tm), :], y0_ref, sem).wait()
    pltpu.make_async_copy(y_ref.at[pl.ds(0, tm), :], y1_ref, sem).wait()
    gc = gc_ref[...]
    o_ref[...] = x2_ref[...] + (gc[:, 0:1] * y0_ref[...] + gc[:, 1:2] * y1_ref[...])


def _combine_call(dest0, dest1, x2, gc, y_rows):
    n = x2.shape[0]
    tm = TM_MOVE
    idx = pl.BlockSpec((tm,), lambda i: (i,), memory_space=pltpu.SMEM)
    return pl.pallas_call(
        _combine_kernel,
        grid=(n // tm,),
        in_specs=[idx, idx, pl.BlockSpec((tm, D_MODEL), lambda i: (i, 0)), pl.BlockSpec((tm, LANES), lambda i: (i, 0)),
                  pl.BlockSpec(memory_space=pl.ANY)],
        out_specs=pl.BlockSpec((tm, D_MODEL), lambda i: (i, 0)),
        out_shape=jax.ShapeDtypeStruct((n, D_MODEL), F32),
        scratch_shapes=[pltpu.VMEM((tm, D_MODEL), F32), pltpu.VMEM((tm, D_MODEL), F32), pltpu.SemaphoreType.DMA],
        compiler_params=pltpu.CompilerParams(dimension_semantics=("arbitrary",), vmem_limit_bytes=VMEM_LIMIT,
                                             disable_bounds_checks=True),
        name="combine",
    )(dest0, dest1, x2, gc, y_rows)


def _prepare(norm_mix, w_in, a_v_norm, a_spatial_w, a_spatial_b, q_norm, k_norm, out_norm_a, out_norm_b, w_out,
             norm_ffn, w_router_group, b_router_group, w_router_expert, b_router_expert,
             w_expert_gate, w_expert_up, w_expert_down):
    ch = jnp.arange(A_WIDTH) // GROUP_DIM
    pad = ROUTER_ROWS - N_EXPERTS - N_GROUPS
    head = jnp.arange(HEADS, dtype=F32).reshape(B_WIDTH // LANES, 2, 1)
    slope_rows = jnp.broadcast_to(2.0 ** (-8.0 * (head + 1.0) / HEADS), (B_WIDTH // LANES, 2, TQ + 2 * HALF))
    slopes = jnp.concatenate([slope_rows, jnp.zeros((B_WIDTH // LANES, 6, TQ + 2 * HALF), F32)], axis=1)
    return dict(
        gmix=norm_mix.reshape(1, D_MODEL),
        win=w_in.astype(BF16),
        bd=(ch[:, None] == ch[None, :]).astype(BF16),
        avn=a_v_norm.reshape(1, A_WIDTH),
        wcat=jnp.concatenate([a_spatial_w[0::2], a_spatial_w[1::2]], axis=2).astype(BF16),
        bias=jnp.repeat(a_spatial_b.T, GROUP_DIM, axis=1),
        gq=(jnp.tile(q_norm, HEADS) * (HEAD_DIM ** -0.5)).reshape(1, B_WIDTH),
        gk=jnp.tile(k_norm, HEADS).reshape(1, B_WIDTH),
        gna=out_norm_a.reshape(1, A_WIDTH),
        gnb=out_norm_b.reshape(1, B_WIDTH),
        slopes=slopes,
        wout=w_out.astype(BF16),
        gffn=norm_ffn.reshape(1, D_MODEL),
        wr=jnp.concatenate([w_router_expert.T, w_router_group.T, jnp.zeros((pad, D_MODEL), F32)], axis=0).astype(BF16),
        br=jnp.concatenate([b_router_expert, b_router_group, jnp.zeros((pad,), F32)]).reshape(ROUTER_ROWS, 1),
        tri=(jnp.arange(TM_OUT)[:, None] < jnp.arange(TM_OUT)[None, :]).astype(BF16),
        wg=w_expert_gate.astype(BF16),
        wu=w_expert_up.astype(BF16),
        wd=w_expert_down.astype(BF16),
    )


def _layer(x, p):
    b, t, _ = x.shape
    n = b * t
    x2d = x.reshape(n, D_MODEL)
    ya, q, k, v = _proj_call(x2d, p["gmix"], p["win"], p["bd"], p["avn"], p["wcat"], p["bias"], p["gq"], p["gk"], p["gna"])
    q, k, v = (a.reshape(b, t, B_WIDTH) for a in (q, k, v))
    att = _attn_call(q, k, v, p["slopes"]).reshape(n, B_WIDTH)
    x2, xn, ei, gc, cnt = _out_call(x2d, ya, att, p["gnb"], p["wout"], p["gffn"], p["wr"], p["br"], p["tri"])

    counts = cnt[:, 0].astype(jnp.int32)
    pcounts = (counts + ROW_BLOCK - 1) // ROW_BLOCK * ROW_BLOCK
    pends = jnp.cumsum(pcounts)
    pstarts = pends - pcounts
    nb = (2 * n) // ROW_BLOCK + N_EXPERTS
    starts = jnp.arange(nb, dtype=jnp.int32) * ROW_BLOCK
    block_e = jnp.minimum(jnp.sum((pends[None, :] <= starts[:, None]).astype(jnp.int32), axis=1), N_EXPERTS - 1)
    block_valid = jnp.clip(pstarts[block_e] + counts[block_e] - starts, 0, ROW_BLOCK)
    dest0 = pstarts[ei[0]] + ei[2]
    dest1 = pstarts[ei[1]] + ei[3]
    used = pends[N_EXPERTS - 1] // ROW_BLOCK
    tail = used + jnp.arange(N_EXPERTS, dtype=jnp.int32)
    zero_blocks = jnp.concatenate([jnp.maximum(pends // ROW_BLOCK - 1, 0), jnp.minimum(tail, nb - 1)]).astype(jnp.int32)
    zero_on = jnp.concatenate([pcounts > 0, tail < nb]).astype(jnp.int32)

    x_rows = _dispatch_call(zero_blocks, zero_on, dest0, dest1, xn, nb * ROW_BLOCK)
    y_rows = _expert_call(block_e, block_valid, x_rows, p["wg"], p["wu"], p["wd"])
    out = _combine_call(dest0, dest1, x2, gc, y_rows)
    return out.reshape(b, t, D_MODEL)


def kernel(x_prompt, x_sample, norm_mix, w_in, a_v_norm, a_spatial_w, a_spatial_b, q_norm, k_norm, out_norm_a,
           out_norm_b, w_out, norm_ffn, w_router_group, b_router_group, w_router_expert, b_router_expert,
           w_expert_gate, w_expert_up, w_expert_down):
    depth = norm_mix.shape[0]
    layers = [
        _prepare(norm_mix[l], w_in[l], a_v_norm[l], a_spatial_w[l], a_spatial_b[l], q_norm[l], k_norm[l],
                 out_norm_a[l], out_norm_b[l], w_out[l], norm_ffn[l], w_router_group[l], b_router_group[l],
                 w_router_expert[l], b_router_expert[l], w_expert_gate[l], w_expert_up[l], w_expert_down[l])
        for l in range(depth)
    ]

    def run(x):
        for p in layers:
            x = _layer(x, p)
        return x

    return (run(x_prompt), run(x_sample))
```

```python
import functools

import jax
import jax.numpy as jnp
from jax import lax
from jax.experimental import pallas as pl
from jax.experimental.pallas import tpu as pltpu

D_MODEL = 1024
A_WIDTH = 512
B_WIDTH = 512
IN_WIDTH = 2 * A_WIDTH + 3 * B_WIDTH
A_GROUPS = 8
GROUP_DIM = 64
CHUNK = 128
HEADS = 8
HEAD_DIM = 64
DILATIONS = (1, 4, 16)
HALF = 64
N_GROUPS = 4
EXPERTS_PER_GROUP = 8
N_EXPERTS = 32
D_EXPERT = 512
ROW_BLOCK = 256
EPS = 1e-6
NEG_INF = -1e30

LANES = 128
ROUTER_ROWS = 48
TM_PROJ = 256
TM_OUT = 512
TM_MOVE = 512
MOVE_UNROLL = 8
TQ = 128
SUPER = TQ * max(DILATIONS)
HALO = HALF * max(DILATIONS)
VMEM_LIMIT = 48 * 1024 * 1024

F32 = jnp.float32
BF16 = jnp.bfloat16
NT_DIMS = (((1,), (1,)), ((), ()))


def _rms(x, gain):
    return x * lax.rsqrt(jnp.mean(x * x, axis=-1, keepdims=True) + EPS) * gain


def _proj_kernel(x_ref, gmix_ref, win_ref, bd_ref, avn_ref, wcat_ref, bias_ref, gq_ref, gk_ref, gna_ref,
                 ya_ref, q_ref, k_ref, v_ref):
    h = _rms(x_ref[...], gmix_ref[...])
    proj = jnp.dot(h.astype(BF16), win_ref[...], preferred_element_type=F32)
    pu = proj[:, 0:A_WIDTH]
    pv = proj[:, A_WIDTH:2 * A_WIDTH]
    q = proj[:, 2 * A_WIDTH:2 * A_WIDTH + B_WIDTH]
    k = proj[:, 2 * A_WIDTH + B_WIDTH:2 * A_WIDTH + 2 * B_WIDTH]
    v = proj[:, 2 * A_WIDTH + 2 * B_WIDTH:]
    bd = bd_ref[...]

    def group_norm(t, gain):
        ms = jnp.dot((t * t).astype(BF16), bd, preferred_element_type=F32) * (1.0 / GROUP_DIM)
        return t * lax.rsqrt(ms + EPS) * gain

    u = jax.nn.gelu(pu)
    vn = group_norm(jax.nn.gelu(pv), avn_ref[...]).astype(BF16)
    lane = lax.broadcasted_iota(jnp.int32, (CHUNK, LANES), 1)
    lo = lane < GROUP_DIM
    zero = jnp.zeros((CHUNK, LANES), BF16)
    chunks = []
    for c in range(x_ref.shape[0] // CHUNK):
        blks = []
        for j in range(A_WIDTH // LANES):
            vb = vn[c * CHUNK:(c + 1) * CHUNK, j * LANES:(j + 1) * LANES]
            rhs = jnp.concatenate([jnp.where(lo, vb, zero), jnp.where(lo, zero, vb)], axis=0)
            blks.append(jnp.dot(wcat_ref[j], rhs, preferred_element_type=F32))
        chunks.append(jnp.concatenate(blks, axis=1) + bias_ref[...])
    mixed = jnp.concatenate(chunks, axis=0)
    ya_ref[...] = _rms(u * mixed, gna_ref[...]).astype(BF16)
    q_ref[...] = group_norm(q, gq_ref[...]).astype(BF16)
    k_ref[...] = group_norm(k, gk_ref[...]).astype(BF16)
    v_ref[...] = v.astype(BF16)


def _proj_call(x2d, gmix, win, bd, avn, wcat, bias, gq, gk, gna):
    n = x2d.shape[0]
    tm = TM_PROJ
    full = lambda shape: pl.BlockSpec(shape, lambda i: (0,) * len(shape))
    tok = lambda w: pl.BlockSpec((tm, w), lambda i: (i, 0))
    return pl.pallas_call(
        _proj_kernel,
        grid=(n // tm,),
        in_specs=[tok(D_MODEL), full((1, D_MODEL)), full((D_MODEL, IN_WIDTH)), full((A_WIDTH, A_WIDTH)),
                  full((1, A_WIDTH)), full((A_WIDTH // LANES, CHUNK, 2 * CHUNK)), full((CHUNK, A_WIDTH)),
                  full((1, B_WIDTH)), full((1, B_WIDTH)), full((1, A_WIDTH))],
        out_specs=[tok(A_WIDTH), tok(B_WIDTH), tok(B_WIDTH), tok(B_WIDTH)],
        out_shape=[jax.ShapeDtypeStruct((n, A_WIDTH), BF16)] + [jax.ShapeDtypeStruct((n, B_WIDTH), BF16)] * 3,
        compiler_params=pltpu.CompilerParams(dimension_semantics=("parallel",), vmem_limit_bytes=VMEM_LIMIT),
        name="proj_gating",
    )(x2d, gmix, win, bd, avn, wcat, bias, gq, gk, gna)


def _attn_kernel(*refs, halo):
    if halo:
        bias_ref, q_ref, kp_ref, kc_ref, kn_ref, vp_ref, vc_ref, vn_ref, o_ref = refs[:9]
        scratch = refs[9:]
    else:
        bias_ref, q_ref, kc_ref, vc_ref, o_ref = refs[:5]
        scratch = refs[5:]
    qf, kf, vf, acc16, m16, l16, acc4, m4, l4 = scratch
    sb = pl.program_id(1)

    qf[...] = q_ref[...].astype(F32)
    if halo:
        kf[0:halo, :] = kp_ref[...].astype(F32)
        kf[halo + SUPER:, :] = kn_ref[...].astype(F32)
        vf[0:halo, :] = vp_ref[...].astype(F32)
        vf[halo + SUPER:, :] = vn_ref[...].astype(F32)
    kf[halo:halo + SUPER, :] = kc_ref[...].astype(F32)
    vf[halo:halo + SUPER, :] = vc_ref[...].astype(F32)

    lane = lax.broadcasted_iota(jnp.int32, (TQ, LANES), 1)
    lo = lane < HEAD_DIM

    def branch_unit(dil, res, qs):
        n_sub = SUPER // dil
        if halo:
            tk = TQ + 2 * HALF
            ks = qs - HALF
        else:
            tk = min(TQ + 2 * HALF, n_sub)
            ks = min(max(qs - HALF, 0), n_sub - tk)
        q_start = dil * qs + res
        k_start = halo + dil * ks + res
        if dil == 1:
            qb = qf[pl.ds(q_start, TQ), :]
            kb = kf[pl.ds(k_start, tk), :]
            vb = vf[pl.ds(k_start, tk), :]
        else:
            qb = qf[pl.ds(q_start, TQ, stride=dil), :]
            kb = kf[pl.ds(k_start, tk, stride=dil), :]
            vb = vf[pl.ds(k_start, tk, stride=dil), :]
        qb = qb.astype(BF16)
        kb = kb.astype(BF16)
        vb = vb.astype(BF16)
        branch = DILATIONS.index(dil)
        if halo:
            off = 1
            edge = qs == 0 or qs + TQ == n_sub
            if edge:
                col = lax.broadcasted_iota(jnp.int32, (1, tk), 1)
                c_lo = jnp.where(sb == 0, HALF - qs, 0)
                c_hi = jnp.where(sb == pl.num_programs(1) - 1, n_sub - qs + HALF, tk)
                in_seq = (col >= c_lo) & (col < c_hi)
        else:
            edge = False
            off = (qs - ks) // HALF
        res_h = []
        for hh in range(2):
            qm = jnp.where(lo if hh == 0 else ~lo, qb, jnp.zeros_like(qb))
            s = lax.dot_general(qm, kb, NT_DIMS, preferred_element_type=F32)
            s = s + bias_ref[branch, off, hh, :, 0:tk]
            if edge:
                s = jnp.where(in_seq, s, NEG_INF)
            m = jnp.max(s, axis=-1, keepdims=True)
            p = jnp.exp(s - m)
            l = jnp.sum(p, axis=-1, keepdims=True)
            pv = jnp.dot(p.astype(BF16), vb, preferred_element_type=F32)
            res_h.append((pv, m, l))
        return tuple(jnp.where(lo, a, b) for a, b in zip(res_h[0], res_h[1]))

    def strided_branch(dil, acc_ref, m_ref, l_ref):
        for u in range(SUPER // TQ):
            res = u % dil
            qs = (u // dil) * TQ
            acc, m, l = branch_unit(dil, res, qs)
            rows = pl.ds(dil * qs + res, TQ, stride=dil)
            acc_ref[rows, :] = acc
            m_ref[rows, :] = m
            l_ref[rows, :] = l

    strided_branch(16, acc16, m16, l16)
    strided_branch(4, acc4, m4, l4)

    for u in range(SUPER // TQ):
        qs = u * TQ
        a1, m1, l1 = branch_unit(1, 0, qs)
        rows = pl.ds(qs, TQ)
        m_4, m_16 = m4[rows, :], m16[rows, :]
        m_all = jnp.maximum(jnp.maximum(m1, m_4), m_16)
        e1 = jnp.exp(m1 - m_all)
        e4 = jnp.exp(m_4 - m_all)
        e16 = jnp.exp(m_16 - m_all)
        num = e1 * a1 + e4 * acc4[rows, :] + e16 * acc16[rows, :]
        den = e1 * l1 + e4 * l4[rows, :] + e16 * l16[rows, :]
        o_ref[rows, :] = num / den


def _attn_call(q, k, v, bias):
    b, t, _ = q.shape
    nsb = t // SUPER
    halo = HALO if nsb > 1 else 0
    nkb = t // HALO
    per = SUPER // HALO
    cur = lambda bb, s, j: (bb, s, j)
    blk = lambda rows, f: pl.BlockSpec((None, rows, LANES), f)
    bias_spec = pl.BlockSpec((None,) + bias.shape[1:], lambda bb, s, j: (j, 0, 0, 0, 0, 0))
    if halo:
        prv = lambda bb, s, j: (bb, jnp.maximum(s * per - 1, 0), j)
        nxt = lambda bb, s, j: (bb, jnp.minimum((s + 1) * per, nkb - 1), j)
        args = [bias, q, k, k, k, v, v, v]
        in_specs = [bias_spec, blk(SUPER, cur), blk(HALO, prv), blk(SUPER, cur), blk(HALO, nxt),
                    blk(HALO, prv), blk(SUPER, cur), blk(HALO, nxt)]
    else:
        args = [bias, q, k, v]
        in_specs = [bias_spec, blk(SUPER, cur), blk(SUPER, cur), blk(SUPER, cur)]
    stat = pltpu.VMEM((SUPER, LANES), F32)
    kv = pltpu.VMEM((SUPER + 2 * halo, LANES), F32)
    return pl.pallas_call(
        functools.partial(_attn_kernel, halo=halo),
        grid=(b, nsb, B_WIDTH // LANES),
        in_specs=in_specs,
        out_specs=blk(SUPER, cur),
        out_shape=jax.ShapeDtypeStruct((b, t, B_WIDTH), F32),
        scratch_shapes=[stat, kv, kv, stat, stat, stat, stat, stat, stat],
        compiler_params=pltpu.CompilerParams(dimension_semantics=("parallel", "parallel", "parallel"),
                                             vmem_limit_bytes=VMEM_LIMIT),
        name="attention",
    )(*args)


def _out_kernel(x_ref, ya_ref, att_ref, gnb_ref, wout_ref, gffn_ref, wr_ref, br_ref, tri_ref,
                x2_ref, ei_ref, gc_ref, cnt_ref, base_ref):
    tm = x_ref.shape[0]

    @pl.when(pl.program_id(0) == 0)
    def _():
        base_ref[...] = jnp.zeros_like(base_ref)

    yb = _rms(att_ref[...], gnb_ref[...]).astype(BF16)
    a = jnp.concatenate([ya_ref[...], yb], axis=1)
    x2 = x_ref[...] + jnp.dot(a, wout_ref[...], preferred_element_type=F32)
    x2_ref[...] = x2
    xn = _rms(x2, gffn_ref[...])

    lg = lax.dot_general(wr_ref[...], xn.astype(BF16), NT_DIMS, preferred_element_type=F32) + br_ref[...]
    e_log = lg[0:N_EXPERTS]
    g_log = lg[N_EXPERTS:N_EXPERTS + N_GROUPS]
    r4 = lax.broadcasted_iota(jnp.int32, (N_GROUPS, tm), 0).astype(F32)
    g_max = jnp.max(g_log, axis=0, keepdims=True)
    g_sel = jnp.min(jnp.where(g_log == g_max, r4, float(N_GROUPS)), axis=0, keepdims=True)
    p_group = 1.0 / jnp.sum(jnp.exp(g_log - g_max), axis=0, keepdims=True)
    e_sel = jnp.zeros((EXPERTS_PER_GROUP, tm), F32)
    for g in range(N_GROUPS):
        e_sel = jnp.where(g_sel == float(g), e_log[g * EXPERTS_PER_GROUP:(g + 1) * EXPERTS_PER_GROUP], e_sel)
    r8 = lax.broadcasted_iota(jnp.int32, (EXPERTS_PER_GROUP, tm), 0).astype(F32)
    v1 = jnp.max(e_sel, axis=0, keepdims=True)
    i1 = jnp.min(jnp.where(e_sel == v1, r8, float(EXPERTS_PER_GROUP)), axis=0, keepdims=True)
    e_rest = jnp.where(r8 == i1, -jnp.inf, e_sel)
    v2 = jnp.max(e_rest, axis=0, keepdims=True)
    i2 = jnp.min(jnp.where(e_rest == v2, r8, float(EXPERTS_PER_GROUP)), axis=0, keepdims=True)
    d = jnp.exp(v2 - v1)
    gate1 = p_group * (1.0 / (1.0 + d))
    gate2 = p_group * (d / (1.0 + d))
    eid1 = g_sel * float(EXPERTS_PER_GROUP) + i1
    eid2 = g_sel * float(EXPERTS_PER_GROUP) + i2

    r32 = lax.broadcasted_iota(jnp.int32, (N_EXPERTS, tm), 0).astype(F32)
    oh1 = r32 == eid1
    oh2 = r32 == eid2
    oh1f = jnp.where(oh1, 1.0, 0.0)
    oh2f = jnp.where(oh2, 1.0, 0.0)
    tri = tri_ref[...]
    pre1 = jnp.dot(oh1f.astype(BF16), tri, preferred_element_type=F32)
    pre2 = jnp.dot(oh2f.astype(BF16), tri, preferred_element_type=F32)
    tot1 = jnp.sum(oh1f, axis=1, keepdims=True)
    tot2 = jnp.sum(oh2f, axis=1, keepdims=True)
    base_full = base_ref[...]
    base = base_full[:, 0:1]
    rank1 = jnp.sum(jnp.where(oh1, base + pre1, 0.0), axis=0, keepdims=True)
    rank2 = jnp.sum(jnp.where(oh2, base + tot1 + pre2, 0.0), axis=0, keepdims=True)
    base_full = base_full + tot1 + tot2
    base_ref[...] = base_full
    cnt_ref[...] = base_full
    ei_ref[...] = jnp.concatenate([eid1, eid2, rank1, rank2], axis=0).astype(jnp.int32)
    r128 = lax.broadcasted_iota(jnp.int32, (LANES, tm), 0)
    gates_rows = jnp.where(r128 == 0, gate1, jnp.where(r128 == 1, gate2, 0.0))
    gc_ref[...] = gates_rows.T


def _out_call(x2d, ya, att, gnb, wout, gffn, wr, br, tri):
    n = x2d.shape[0]
    tm = TM_OUT
    full = lambda shape: pl.BlockSpec(shape, lambda i: (0,) * len(shape))
    tok = lambda w: pl.BlockSpec((tm, w), lambda i: (i, 0))
    return pl.pallas_call(
        _out_kernel,
        grid=(n // tm,),
        in_specs=[tok(D_MODEL), tok(A_WIDTH), tok(B_WIDTH), full((1, B_WIDTH)), full((D_MODEL, D_MODEL)),
                  full((1, D_MODEL)), full((ROUTER_ROWS, D_MODEL)), full((ROUTER_ROWS, 1)), full((tm, tm))],
        out_specs=[tok(D_MODEL), pl.BlockSpec((4, tm), lambda i: (0, i)), tok(LANES),
                   full((N_EXPERTS, LANES))],
        out_shape=[jax.ShapeDtypeStruct((n, D_MODEL), F32),
                   jax.ShapeDtypeStruct((4, n), jnp.int32), jax.ShapeDtypeStruct((n, LANES), F32),
                   jax.ShapeDtypeStruct((N_EXPERTS, LANES), F32)],
        scratch_shapes=[pltpu.VMEM((N_EXPERTS, LANES), F32)],
        compiler_params=pltpu.CompilerParams(dimension_semantics=("arbitrary",), vmem_limit_bytes=VMEM_LIMIT),
        name="out_router",
    )(x2d, ya, att, gnb, wout, gffn, wr, br, tri)


def _row_copy(src, s, dst, d, sem):
    return pltpu.make_async_copy(src.at[pl.ds(s, 1), :], dst.at[pl.ds(d, 1), :], sem)


def _dispatch_kernel(zb_ref, zon_ref, d0_ref, d1_ref, x2_ref, gffn_ref, rows_ref, xn_ref, zero_ref, sem, zsem):
    tm = x2_ref.shape[0]
    xn_ref[...] = _rms(x2_ref[...], gffn_ref[...])

    @pl.when(pl.program_id(0) == 0)
    def _():
        zero_ref[...] = jnp.zeros_like(zero_ref)

        def zero_copy(i):
            start = pl.multiple_of(zb_ref[i] * ROW_BLOCK, ROW_BLOCK)
            return pltpu.make_async_copy(zero_ref, rows_ref.at[pl.ds(start, ROW_BLOCK), :], zsem)

        def start(i, c):
            @pl.when(zon_ref[i] == 1)
            def _():
                zero_copy(i).start()
            return c

        def wait(i, c):
            @pl.when(zon_ref[i] == 1)
            def _():
                zero_copy(i).wait()
            return c

        lax.fori_loop(0, 2 * N_EXPERTS, start, 0)
        lax.fori_loop(0, 2 * N_EXPERTS, wait, 0)

    def issue(g, c):
        for i in range(MOVE_UNROLL):
            t = g * MOVE_UNROLL + i
            _row_copy(xn_ref, t, rows_ref, d0_ref[t], sem).start()
            _row_copy(xn_ref, t, rows_ref, d1_ref[t], sem).start()
        return c

    lax.fori_loop(0, tm // MOVE_UNROLL, issue, 0)
    tile = pltpu.make_async_copy(xn_ref, rows_ref.at[pl.ds(0, tm), :], sem)
    tile.wait()
    tile.wait()


def _dispatch_call(zero_blocks, zero_on, dest0, dest1, x2, gffn, n_rows):
    n = x2.shape[0]
    tm = TM_MOVE
    idx = pl.BlockSpec((tm,), lambda i, zb, zon: (i,), memory_space=pltpu.SMEM)
    grid_spec = pltpu.PrefetchScalarGridSpec(
        num_scalar_prefetch=2,
        grid=(n // tm,),
        in_specs=[idx, idx, pl.BlockSpec((tm, D_MODEL), lambda i, zb, zon: (i, 0)),
                  pl.BlockSpec((1, D_MODEL), lambda i, zb, zon: (0, 0))],
        out_specs=pl.BlockSpec(memory_space=pl.ANY),
        scratch_shapes=[pltpu.VMEM((tm, D_MODEL), F32), pltpu.VMEM((ROW_BLOCK, D_MODEL), F32),
                        pltpu.SemaphoreType.DMA, pltpu.SemaphoreType.DMA],
    )
    return pl.pallas_call(
        _dispatch_kernel,
        grid_spec=grid_spec,
        out_shape=jax.ShapeDtypeStruct((n_rows, D_MODEL), F32),
        compiler_params=pltpu.CompilerParams(dimension_semantics=("arbitrary",), has_side_effects=True,
                                             disable_bounds_checks=True),
        name="dispatch",
    )(zero_blocks, zero_on, dest0, dest1, x2, gffn)


def _expert_kernel(be_ref, bv_ref, x_ref, wg_ref, wu_ref, wd_ref, y_ref):
    del be_ref
    valid = bv_ref[pl.program_id(0)]

    @pl.when(valid > 0)
    def _():
        xb = x_ref[...].astype(BF16)
        g = jnp.dot(xb, wg_ref[...], preferred_element_type=F32)
        u = jnp.dot(xb, wu_ref[...], preferred_element_type=F32)
        h = (jax.nn.silu(g) * u).astype(BF16)
        y_ref[...] = jnp.dot(h, wd_ref[...], preferred_element_type=F32)

    @pl.when(valid == 0)
    def _():
        y_ref[...] = jnp.zeros_like(y_ref)


def _expert_call(block_e, block_valid, x_rows, wg, wu, wd):
    n_rows = x_rows.shape[0]
    nb = n_rows // ROW_BLOCK
    grid_spec = pltpu.PrefetchScalarGridSpec(
        num_scalar_prefetch=2,
        grid=(nb,),
        in_specs=[pl.BlockSpec((ROW_BLOCK, D_MODEL), lambda b, be, bv: (b, 0)),
                  pl.BlockSpec((None, D_MODEL, D_EXPERT), lambda b, be, bv: (be[b], 0, 0)),
                  pl.BlockSpec((None, D_MODEL, D_EXPERT), lambda b, be, bv: (be[b], 0, 0)),
                  pl.BlockSpec((None, D_EXPERT, D_MODEL), lambda b, be, bv: (be[b], 0, 0))],
        out_specs=pl.BlockSpec((ROW_BLOCK, D_MODEL), lambda b, be, bv: (b, 0)),
    )
    return pl.pallas_call(
        _expert_kernel,
        grid_spec=grid_spec,
        out_shape=jax.ShapeDtypeStruct((n_rows, D_MODEL), F32),
        compiler_params=pltpu.CompilerParams(dimension_semantics=("arbitrary",), vmem_limit_bytes=VMEM_LIMIT),
        name="experts",
    )(block_e, block_valid, x_rows, wg, wu, wd)


def _combine_kernel(d0_ref, d1_ref, x2_ref, gc_ref, y_ref, o_ref, y0_ref, y1_ref, sem):
    tm = x2_ref.shape[0]

    def issue(g, c):
        for i in range(MOVE_UNROLL):
            t = g * MOVE_UNROLL + i
            _row_copy(y_ref, d0_ref[t], y0_ref, t, sem).start()
            _row_copy(y_ref, d1_ref[t], y1_ref, t, sem).start()
        return c

    lax.fori_loop(0, tm // MOVE_UNROLL, issue, 0)
    pltpu.make_async_copy(y_ref.at[pl.ds(0, tm), :], y0_ref, sem).wait()
    pltpu.make_async_copy(y_ref.at[pl.ds(0, tm), :], y1_ref, sem).wait()
    gc = gc_ref[...]
    o_ref[...] = x2_ref[...] + (gc[:, 0:1] * y0_ref[...] + gc[:, 1:2] * y1_ref[...])


def _combine_call(dest0, dest1, x2, gc, y_rows):
    n = x2.shape[0]
    tm = TM_MOVE
    idx = pl.BlockSpec((tm,), lambda i: (i,), memory_space=pltpu.SMEM)
    return pl.pallas_call(
        _combine_kernel,
        grid=(n // tm,),
        in_specs=[idx, idx, pl.BlockSpec((tm, D_MODEL), lambda i: (i, 0)), pl.BlockSpec((tm, LANES), lambda i: (i, 0)),
                  pl.BlockSpec(memory_space=pl.ANY)],
        out_specs=pl.BlockSpec((tm, D_MODEL), lambda i: (i, 0)),
        out_shape=jax.ShapeDtypeStruct((n, D_MODEL), F32),
        scratch_shapes=[pltpu.VMEM((tm, D_MODEL), F32), pltpu.VMEM((tm, D_MODEL), F32), pltpu.SemaphoreType.DMA],
        compiler_params=pltpu.CompilerParams(dimension_semantics=("arbitrary",), vmem_limit_bytes=VMEM_LIMIT,
                                             disable_bounds_checks=True),
        name="combine",
    )(dest0, dest1, x2, gc, y_rows)


def _prepare(norm_mix, w_in, a_v_norm, a_spatial_w, a_spatial_b, q_norm, k_norm, out_norm_a, out_norm_b, w_out,
             norm_ffn, w_router_group, b_router_group, w_router_expert, b_router_expert,
             w_expert_gate, w_expert_up, w_expert_down):
    ch = jnp.arange(A_WIDTH) // GROUP_DIM
    pad = ROUTER_ROWS - N_EXPERTS - N_GROUPS
    tk = TQ + 2 * HALF
    rel = (jnp.arange(tk)[None, None, :] - jnp.arange(TQ)[None, :, None]
           - HALF * jnp.arange(3)[:, None, None])
    dist = jnp.abs(rel).astype(F32)[None, :, None]
    slope = 2.0 ** (-8.0 * (jnp.arange(HEADS, dtype=F32) + 1.0) / HEADS)
    dil = jnp.asarray(DILATIONS, F32)
    abias = jnp.where(dist <= HALF,
                      -slope[None, None, :, None, None] * (dist * dil[:, None, None, None, None]),
                      NEG_INF)
    abias = abias.reshape(len(DILATIONS), 3, B_WIDTH // LANES, 2, TQ, tk).transpose(2, 0, 1, 3, 4, 5)
    return dict(
        gmix=norm_mix.reshape(1, D_MODEL),
        win=w_in.astype(BF16),
        bd=(ch[:, None] == ch[None, :]).astype(BF16),
        avn=a_v_norm.reshape(1, A_WIDTH),
        wcat=jnp.concatenate([a_spatial_w[0::2], a_spatial_w[1::2]], axis=2).astype(BF16),
        bias=jnp.repeat(a_spatial_b.T, GROUP_DIM, axis=1),
        gq=(jnp.tile(q_norm, HEADS) * (HEAD_DIM ** -0.5)).reshape(1, B_WIDTH),
        gk=jnp.tile(k_norm, HEADS).reshape(1, B_WIDTH),
        gna=out_norm_a.reshape(1, A_WIDTH),
        gnb=out_norm_b.reshape(1, B_WIDTH),
        abias=abias,
        wout=w_out.astype(BF16),
        gffn=norm_ffn.reshape(1, D_MODEL),
        wr=jnp.concatenate([w_router_expert.T, w_router_group.T, jnp.zeros((pad, D_MODEL), F32)], axis=0).astype(BF16),
        br=jnp.concatenate([b_router_expert, b_router_group, jnp.zeros((pad,), F32)]).reshape(ROUTER_ROWS, 1),
        tri=(jnp.arange(TM_OUT)[:, None] < jnp.arange(TM_OUT)[None, :]).astype(BF16),
        wg=w_expert_gate.astype(BF16),
        wu=w_expert_up.astype(BF16),
        wd=w_expert_down.astype(BF16),
    )


def _layer(x, p):
    b, t, _ = x.shape
    n = b * t
    x2d = x.reshape(n, D_MODEL)
    ya, q, k, v = _proj_call(x2d, p["gmix"], p["win"], p["bd"], p["avn"], p["wcat"], p["bias"], p["gq"], p["gk"], p["gna"])
    q, k, v = (a.reshape(b, t, B_WIDTH) for a in (q, k, v))
    att = _attn_call(q, k, v, p["abias"]).reshape(n, B_WIDTH)
    x2, ei, gc, cnt = _out_call(x2d, ya, att, p["gnb"], p["wout"], p["gffn"], p["wr"], p["br"], p["tri"])

    counts = cnt[:, 0].astype(jnp.int32)
    pcounts = (counts + ROW_BLOCK - 1) // ROW_BLOCK * ROW_BLOCK
    pends = jnp.cumsum(pcounts)
    pstarts = pends - pcounts
    nb = (2 * n) // ROW_BLOCK + N_EXPERTS
    starts = jnp.arange(nb, dtype=jnp.int32) * ROW_BLOCK
    block_e = jnp.minimum(jnp.sum((pends[None, :] <= starts[:, None]).astype(jnp.int32), axis=1), N_EXPERTS - 1)
    block_valid = jnp.clip(pstarts[block_e] + counts[block_e] - starts, 0, ROW_BLOCK)
    dest0 = pstarts[ei[0]] + ei[2]
    dest1 = pstarts[ei[1]] + ei[3]
    used = pends[N_EXPERTS - 1] // ROW_BLOCK
    tail = used + jnp.arange(N_EXPERTS, dtype=jnp.int32)
    zero_blocks = jnp.concatenate([jnp.maximum(pends // ROW_BLOCK - 1, 0), jnp.minimum(tail, nb - 1)]).astype(jnp.int32)
    zero_on = jnp.concatenate([pcounts > 0, tail < nb]).astype(jnp.int32)

    x_rows = _dispatch_call(zero_blocks, zero_on, dest0, dest1, x2, p["gffn"], nb * ROW_BLOCK)
    y_rows = _expert_call(block_e, block_valid, x_rows, p["wg"], p["wu"], p["wd"])
    out = _combine_call(dest0, dest1, x2, gc, y_rows)
    return out.reshape(b, t, D_MODEL)


def kernel(x_prompt, x_sample, norm_mix, w_in, a_v_norm, a_spatial_w, a_spatial_b, q_norm, k_norm, out_norm_a,
           out_norm_b, w_out, norm_ffn, w_router_group, b_router_group, w_router_expert, b_router_expert,
           w_expert_gate, w_expert_up, w_expert_down):
    depth = norm_mix.shape[0]
    layers = [
        _prepare(norm_mix[l], w_in[l], a_v_norm[l], a_spatial_w[l], a_spatial_b[l], q_norm[l], k_norm[l],
                 out_norm_a[l], out_norm_b[l], w_out[l], norm_ffn[l], w_router_group[l], b_router_group[l],
                 w_router_expert[l], b_router_expert[l], w_expert_gate[l], w_expert_up[l], w_expert_down[l])
        for l in range(depth)
    ]

    def run(x):
        for p in layers:
            x = _layer(x, p)
        return x

    return (run(x_prompt), run(x_sample))
```

```python
import jax
import jax.numpy as jnp
from jax import lax
from jax.experimental import pallas as pl
from jax.experimental.pallas import tpu as pltpu

D_MODEL = 1024
A_WIDTH = 512
B_WIDTH = 512
IN_WIDTH = 2 * A_WIDTH + 3 * B_WIDTH
A_GROUPS = 8
GROUP_DIM = 64
CHUNK = 128
HEADS = 8
HEAD_DIM = 64
DILATIONS = (1, 4, 16)
HALF = 64
N_GROUPS = 4
EXPERTS_PER_GROUP = 8
N_EXPERTS = 32
D_EXPERT = 512
ROW_BLOCK = 256
EPS = 1e-6
NEG_INF = -1e30

LANES = 128
ROUTER_ROWS = 48
TM_PROJ = 512
TM_OUT = 512
TM_MOVE = 512
MOVE_UNROLL = 8
TQ = 128
TK = TQ + 2 * HALF
SUPER = TQ * max(DILATIONS)
HALO = HALF * max(DILATIONS)
VMEM_LIMIT = 48 * 1024 * 1024

F32 = jnp.float32
BF16 = jnp.bfloat16
NT_DIMS = (((1,), (1,)), ((), ()))


def _rms(x, gain):
    return x * lax.rsqrt(jnp.mean(x * x, axis=-1, keepdims=True) + EPS) * gain


def _proj_kernel(x_ref, gmix_ref, win_ref, bd_ref, avn_ref, wcat_ref, bias_ref, gq_ref, gk_ref, gna_ref,
                 ya_ref, q_ref, k_ref, v_ref):
    h = _rms(x_ref[...], gmix_ref[...])
    proj = jnp.dot(h.astype(BF16), win_ref[...], preferred_element_type=F32)
    pu = proj[:, 0:A_WIDTH]
    pv = proj[:, A_WIDTH:2 * A_WIDTH]
    q = proj[:, 2 * A_WIDTH:2 * A_WIDTH + B_WIDTH]
    k = proj[:, 2 * A_WIDTH + B_WIDTH:2 * A_WIDTH + 2 * B_WIDTH]
    v = proj[:, 2 * A_WIDTH + 2 * B_WIDTH:]
    bd = bd_ref[...]

    def group_norm(t, gain):
        ms = jnp.dot((t * t).astype(BF16), bd, preferred_element_type=F32) * (1.0 / GROUP_DIM)
        return t * lax.rsqrt(ms + EPS) * gain

    u = jax.nn.gelu(pu)
    vn = group_norm(jax.nn.gelu(pv), avn_ref[...]).astype(BF16)
    lane = lax.broadcasted_iota(jnp.int32, (CHUNK, LANES), 1)
    lo = lane < GROUP_DIM
    zero = jnp.zeros((CHUNK, LANES), BF16)
    chunks = []
    for c in range(x_ref.shape[0] // CHUNK):
        blks = []
        for j in range(A_WIDTH // LANES):
            vb = vn[c * CHUNK:(c + 1) * CHUNK, j * LANES:(j + 1) * LANES]
            rhs = jnp.concatenate([jnp.where(lo, vb, zero), jnp.where(lo, zero, vb)], axis=0)
            blks.append(jnp.dot(wcat_ref[j], rhs, preferred_element_type=F32))
        chunks.append(jnp.concatenate(blks, axis=1) + bias_ref[...])
    mixed = jnp.concatenate(chunks, axis=0)
    ya_ref[...] = _rms(u * mixed, gna_ref[...]).astype(BF16)
    q_ref[...] = group_norm(q, gq_ref[...]).astype(BF16)
    k_ref[...] = group_norm(k, gk_ref[...]).astype(BF16)
    v_ref[...] = v.astype(BF16)


def _proj_call(x2d, gmix, win, bd, avn, wcat, bias, gq, gk, gna):
    n = x2d.shape[0]
    tm = TM_PROJ
    full = lambda shape: pl.BlockSpec(shape, lambda i: (0,) * len(shape))
    tok = lambda w: pl.BlockSpec((tm, w), lambda i: (i, 0))
    return pl.pallas_call(
        _proj_kernel,
        grid=(n // tm,),
        in_specs=[tok(D_MODEL), full((1, D_MODEL)), full((D_MODEL, IN_WIDTH)), full((A_WIDTH, A_WIDTH)),
                  full((1, A_WIDTH)), full((A_WIDTH // LANES, CHUNK, 2 * CHUNK)), full((CHUNK, A_WIDTH)),
                  full((1, B_WIDTH)), full((1, B_WIDTH)), full((1, A_WIDTH))],
        out_specs=[tok(A_WIDTH), tok(B_WIDTH), tok(B_WIDTH), tok(B_WIDTH)],
        out_shape=[jax.ShapeDtypeStruct((n, A_WIDTH), BF16)] + [jax.ShapeDtypeStruct((n, B_WIDTH), BF16)] * 3,
        compiler_params=pltpu.CompilerParams(dimension_semantics=("parallel",), vmem_limit_bytes=VMEM_LIMIT),
        name="proj_gating",
    )(x2d, gmix, win, bd, avn, wcat, bias, gq, gk, gna)


def _attn_kernel(bias_ref, q_ref, kp_ref, kc_ref, kn_ref, vp_ref, vc_ref, vn_ref, o_ref,
                 qf, kf, vf, acc16, m16, l16, acc4, m4, l4):
    sb = pl.program_id(1)

    qf[...] = q_ref[...].astype(F32)
    kf[0:HALO, :] = kp_ref[...].astype(F32)
    kf[HALO:HALO + SUPER, :] = kc_ref[...].astype(F32)
    kf[HALO + SUPER:, :] = kn_ref[...].astype(F32)
    vf[0:HALO, :] = vp_ref[...].astype(F32)
    vf[HALO:HALO + SUPER, :] = vc_ref[...].astype(F32)
    vf[HALO + SUPER:, :] = vn_ref[...].astype(F32)

    lane = lax.broadcasted_iota(jnp.int32, (TQ, LANES), 1)
    lo = lane < HEAD_DIM

    def branch_unit(dil, res, qs):
        n_sub = SUPER // dil
        q_start = dil * qs + res
        k_start = HALO + dil * (qs - HALF) + res
        if dil == 1:
            qb = qf[pl.ds(q_start, TQ), :]
            kb = kf[pl.ds(k_start, TK), :]
            vb = vf[pl.ds(k_start, TK), :]
        else:
            qb = qf[pl.ds(q_start, TQ, stride=dil), :]
            kb = kf[pl.ds(k_start, TK, stride=dil), :]
            vb = vf[pl.ds(k_start, TK, stride=dil), :]
        qb = qb.astype(BF16)
        kb = kb.astype(BF16)
        vb = vb.astype(BF16)
        branch = DILATIONS.index(dil)
        edge = qs == 0 or qs + TQ == n_sub
        if edge:
            col = lax.broadcasted_iota(jnp.int32, (1, TK), 1)
            c_lo = jnp.where(sb == 0, HALF - qs, 0)
            c_hi = jnp.where(sb == pl.num_programs(1) - 1, n_sub - qs + HALF, TK)
            in_seq = (col >= c_lo) & (col < c_hi)
        res_h = []
        for hh in range(2):
            qm = jnp.where(lo if hh == 0 else ~lo, qb, jnp.zeros_like(qb))
            s = lax.dot_general(qm, kb, NT_DIMS, preferred_element_type=F32)
            s = s + bias_ref[branch, hh]
            if edge:
                s = jnp.where(in_seq, s, NEG_INF)
            m = jnp.max(s, axis=-1, keepdims=True)
            p = jnp.exp(s - m)
            l = jnp.sum(p, axis=-1, keepdims=True)
            pv = jnp.dot(p.astype(BF16), vb, preferred_element_type=F32)
            res_h.append((pv, m, l))
        return tuple(jnp.where(lo, a, b) for a, b in zip(res_h[0], res_h[1]))

    def strided_branch(dil, acc_ref, m_ref, l_ref):
        for u in range(SUPER // TQ):
            res = u % dil
            qs = (u // dil) * TQ
            acc, m, l = branch_unit(dil, res, qs)
            rows = pl.ds(dil * qs + res, TQ, stride=dil)
            acc_ref[rows, :] = acc
            m_ref[rows, :] = m
            l_ref[rows, :] = l

    strided_branch(16, acc16, m16, l16)
    strided_branch(4, acc4, m4, l4)

    for u in range(SUPER // TQ):
        qs = u * TQ
        a1, m1, l1 = branch_unit(1, 0, qs)
        rows = pl.ds(qs, TQ)
        m_4, m_16 = m4[rows, :], m16[rows, :]
        m_all = jnp.maximum(jnp.maximum(m1, m_4), m_16)
        e1 = jnp.exp(m1 - m_all)
        e4 = jnp.exp(m_4 - m_all)
        e16 = jnp.exp(m_16 - m_all)
        num = e1 * a1 + e4 * acc4[rows, :] + e16 * acc16[rows, :]
        den = e1 * l1 + e4 * l4[rows, :] + e16 * l16[rows, :]
        o_ref[rows, :] = num / den


def _attn_call(q, k, v, bias):
    b, t, _ = q.shape
    nsb = t // SUPER
    nkb = t // HALO
    per = SUPER // HALO
    cur = lambda bb, s, j: (bb, s, j)
    prv = lambda bb, s, j: (bb, jnp.maximum(s * per - 1, 0), j)
    nxt = lambda bb, s, j: (bb, jnp.minimum((s + 1) * per, nkb - 1), j)
    blk = lambda rows, f: pl.BlockSpec((None, rows, LANES), f)
    bias_spec = pl.BlockSpec((None,) + bias.shape[1:], lambda bb, s, j: (j, 0, 0, 0, 0))
    stat = pltpu.VMEM((SUPER, LANES), F32)
    kv = pltpu.VMEM((SUPER + 2 * HALO, LANES), F32)
    return pl.pallas_call(
        _attn_kernel,
        grid=(b, nsb, B_WIDTH // LANES),
        in_specs=[bias_spec, blk(SUPER, cur), blk(HALO, prv), blk(SUPER, cur), blk(HALO, nxt),
                  blk(HALO, prv), blk(SUPER, cur), blk(HALO, nxt)],
        out_specs=blk(SUPER, cur),
        out_shape=jax.ShapeDtypeStruct((b, t, B_WIDTH), F32),
        scratch_shapes=[stat, kv, kv, stat, stat, stat, stat, stat, stat],
        compiler_params=pltpu.CompilerParams(dimension_semantics=("parallel", "parallel", "parallel"),
                                             vmem_limit_bytes=VMEM_LIMIT),
        name="attention",
    )(bias, q, k, k, k, v, v, v)


def _out_kernel(x_ref, ya_ref, att_ref, gnb_ref, wout_ref, gffn_ref, wr_ref, br_ref, tri_ref,
                x2_ref, ei_ref, gc_ref, cnt_ref, base_ref):
    tm = x_ref.shape[0]

    @pl.when(pl.program_id(0) == 0)
    def _():
        base_ref[...] = jnp.zeros_like(base_ref)

    yb = _rms(att_ref[...], gnb_ref[...]).astype(BF16)
    a = jnp.concatenate([ya_ref[...], yb], axis=1)
    x2 = x_ref[...] + jnp.dot(a, wout_ref[...], preferred_element_type=F32)
    x2_ref[...] = x2
    xn = _rms(x2, gffn_ref[...])

    lg = lax.dot_general(wr_ref[...], xn.astype(BF16), NT_DIMS, preferred_element_type=F32) + br_ref[...]
    e_log = lg[0:N_EXPERTS]
    g_log = lg[N_EXPERTS:N_EXPERTS + N_GROUPS]
    r4 = lax.broadcasted_iota(jnp.int32, (N_GROUPS, tm), 0).astype(F32)
    g_max = jnp.max(g_log, axis=0, keepdims=True)
    g_sel = jnp.min(jnp.where(g_log == g_max, r4, float(N_GROUPS)), axis=0, keepdims=True)
    p_group = 1.0 / jnp.sum(jnp.exp(g_log - g_max), axis=0, keepdims=True)
    e_sel = jnp.zeros((EXPERTS_PER_GROUP, tm), F32)
    for g in range(N_GROUPS):
        e_sel = jnp.where(g_sel == float(g), e_log[g * EXPERTS_PER_GROUP:(g + 1) * EXPERTS_PER_GROUP], e_sel)
    r8 = lax.broadcasted_iota(jnp.int32, (EXPERTS_PER_GROUP, tm), 0).astype(F32)
    v1 = jnp.max(e_sel, axis=0, keepdims=True)
    i1 = jnp.min(jnp.where(e_sel == v1, r8, float(EXPERTS_PER_GROUP)), axis=0, keepdims=True)
    e_rest = jnp.where(r8 == i1, -jnp.inf, e_sel)
    v2 = jnp.max(e_rest, axis=0, keepdims=True)
    i2 = jnp.min(jnp.where(e_rest == v2, r8, float(EXPERTS_PER_GROUP)), axis=0, keepdims=True)
    d = jnp.exp(v2 - v1)
    gate1 = p_group * (1.0 / (1.0 + d))
    gate2 = p_group * (d / (1.0 + d))
    eid1 = g_sel * float(EXPERTS_PER_GROUP) + i1
    eid2 = g_sel * float(EXPERTS_PER_GROUP) + i2

    r32 = lax.broadcasted_iota(jnp.int32, (N_EXPERTS, tm), 0).astype(F32)
    oh1 = r32 == eid1
    oh2 = r32 == eid2
    oh1f = jnp.where(oh1, 1.0, 0.0)
    oh2f = jnp.where(oh2, 1.0, 0.0)
    tri = tri_ref[...]
    pre1 = jnp.dot(oh1f.astype(BF16), tri, preferred_element_type=F32)
    pre2 = jnp.dot(oh2f.astype(BF16), tri, preferred_element_type=F32)
    tot1 = jnp.sum(oh1f, axis=1, keepdims=True)
    tot2 = jnp.sum(oh2f, axis=1, keepdims=True)
    base_full = base_ref[...]
    base = base_full[:, 0:1]
    rank1 = jnp.sum(jnp.where(oh1, base + pre1, 0.0), axis=0, keepdims=True)
    rank2 = jnp.sum(jnp.where(oh2, base + tot1 + pre2, 0.0), axis=0, keepdims=True)
    base_full = base_full + tot1 + tot2
    base_ref[...] = base_full
    cnt_ref[...] = base_full
    ei_ref[...] = jnp.concatenate([eid1, eid2, rank1, rank2], axis=0).astype(jnp.int32)
    r128 = lax.broadcasted_iota(jnp.int32, (LANES, tm), 0)
    gates_rows = jnp.where(r128 == 0, gate1, jnp.where(r128 == 1, gate2, 0.0))
    gc_ref[...] = gates_rows.T


def _out_call(x2d, ya, att, gnb, wout, gffn, wr, br, tri):
    n = x2d.shape[0]
    tm = TM_OUT
    full = lambda shape: pl.BlockSpec(shape, lambda i: (0,) * len(shape))
    tok = lambda w: pl.BlockSpec((tm, w), lambda i: (i, 0))
    return pl.pallas_call(
        _out_kernel,
        grid=(n // tm,),
        in_specs=[tok(D_MODEL), tok(A_WIDTH), tok(B_WIDTH), full((1, B_WIDTH)), full((D_MODEL, D_MODEL)),
                  full((1, D_MODEL)), full((ROUTER_ROWS, D_MODEL)), full((ROUTER_ROWS, 1)), full((tm, tm))],
        out_specs=[tok(D_MODEL), pl.BlockSpec((4, tm), lambda i: (0, i)), tok(LANES),
                   full((N_EXPERTS, LANES))],
        out_shape=[jax.ShapeDtypeStruct((n, D_MODEL), F32),
                   jax.ShapeDtypeStruct((4, n), jnp.int32), jax.ShapeDtypeStruct((n, LANES), F32),
                   jax.ShapeDtypeStruct((N_EXPERTS, LANES), F32)],
        scratch_shapes=[pltpu.VMEM((N_EXPERTS, LANES), F32)],
        compiler_params=pltpu.CompilerParams(dimension_semantics=("arbitrary",), vmem_limit_bytes=VMEM_LIMIT),
        name="out_router",
    )(x2d, ya, att, gnb, wout, gffn, wr, br, tri)


def _row_copy(src, s, dst, d, sem):
    return pltpu.make_async_copy(src.at[pl.ds(s, 1), :], dst.at[pl.ds(d, 1), :], sem)


def _dispatch_kernel(zb_ref, zon_ref, d0_ref, d1_ref, x2_ref, gffn_ref, rows_ref, xn_ref, zero_ref, sem, zsem):
    tm = x2_ref.shape[0]
    xn_ref[...] = _rms(x2_ref[...], gffn_ref[...])

    @pl.when(pl.program_id(0) == 0)
    def _():
        zero_ref[...] = jnp.zeros_like(zero_ref)

        def zero_copy(i):
            start = pl.multiple_of(zb_ref[i] * ROW_BLOCK, ROW_BLOCK)
            return pltpu.make_async_copy(zero_ref, rows_ref.at[pl.ds(start, ROW_BLOCK), :], zsem)

        def start(i, c):
            @pl.when(zon_ref[i] == 1)
            def _():
                zero_copy(i).start()
            return c

        def wait(i, c):
            @pl.when(zon_ref[i] == 1)
            def _():
                zero_copy(i).wait()
            return c

        lax.fori_loop(0, 2 * N_EXPERTS, start, 0)
        lax.fori_loop(0, 2 * N_EXPERTS, wait, 0)

    def issue(g, c):
        for i in range(MOVE_UNROLL):
            t = g * MOVE_UNROLL + i
            _row_copy(xn_ref, t, rows_ref, d0_ref[t], sem).start()
            _row_copy(xn_ref, t, rows_ref, d1_ref[t], sem).start()
        return c

    lax.fori_loop(0, tm // MOVE_UNROLL, issue, 0)
    tile = pltpu.make_async_copy(xn_ref, rows_ref.at[pl.ds(0, tm), :], sem)
    tile.wait()
    tile.wait()


def _dispatch_call(zero_blocks, zero_on, dest0, dest1, x2, gffn, n_rows):
    n = x2.shape[0]
    tm = TM_MOVE
    idx = pl.BlockSpec((tm,), lambda i, zb, zon: (i,), memory_space=pltpu.SMEM)
    grid_spec = pltpu.PrefetchScalarGridSpec(
        num_scalar_prefetch=2,
        grid=(n // tm,),
        in_specs=[idx, idx, pl.BlockSpec((tm, D_MODEL), lambda i, zb, zon: (i, 0)),
                  pl.BlockSpec((1, D_MODEL), lambda i, zb, zon: (0, 0))],
        out_specs=pl.BlockSpec(memory_space=pl.ANY),
        scratch_shapes=[pltpu.VMEM((tm, D_MODEL), F32), pltpu.VMEM((ROW_BLOCK, D_MODEL), F32),
                        pltpu.SemaphoreType.DMA, pltpu.SemaphoreType.DMA],
    )
    return pl.pallas_call(
        _dispatch_kernel,
        grid_spec=grid_spec,
        out_shape=jax.ShapeDtypeStruct((n_rows, D_MODEL), F32),
        compiler_params=pltpu.CompilerParams(dimension_semantics=("arbitrary",), has_side_effects=True,
                                             disable_bounds_checks=True),
        name="dispatch",
    )(zero_blocks, zero_on, dest0, dest1, x2, gffn)


def _expert_kernel(be_ref, bv_ref, x_ref, wg_ref, wu_ref, wd_ref, y_ref):
    del be_ref
    valid = bv_ref[pl.program_id(0)]

    @pl.when(valid > 0)
    def _():
        xb = x_ref[...].astype(BF16)
        g = jnp.dot(xb, wg_ref[...], preferred_element_type=F32)
        u = jnp.dot(xb, wu_ref[...], preferred_element_type=F32)
        h = (jax.nn.silu(g) * u).astype(BF16)
        y_ref[...] = jnp.dot(h, wd_ref[...], preferred_element_type=F32)

    @pl.when(valid == 0)
    def _():
        y_ref[...] = jnp.zeros_like(y_ref)


def _expert_call(block_e, block_valid, x_rows, wg, wu, wd):
    n_rows = x_rows.shape[0]
    nb = n_rows // ROW_BLOCK
    grid_spec = pltpu.PrefetchScalarGridSpec(
        num_scalar_prefetch=2,
        grid=(nb,),
        in_specs=[pl.BlockSpec((ROW_BLOCK, D_MODEL), lambda b, be, bv: (b, 0)),
                  pl.BlockSpec((None, D_MODEL, D_EXPERT), lambda b, be, bv: (be[b], 0, 0)),
                  pl.BlockSpec((None, D_MODEL, D_EXPERT), lambda b, be, bv: (be[b], 0, 0)),
                  pl.BlockSpec((None, D_EXPERT, D_MODEL), lambda b, be, bv: (be[b], 0, 0))],
        out_specs=pl.BlockSpec((ROW_BLOCK, D_MODEL), lambda b, be, bv: (b, 0)),
    )
    return pl.pallas_call(
        _expert_kernel,
        grid_spec=grid_spec,
        out_shape=jax.ShapeDtypeStruct((n_rows, D_MODEL), F32),
        compiler_params=pltpu.CompilerParams(dimension_semantics=("arbitrary",), vmem_limit_bytes=VMEM_LIMIT),
        name="experts",
    )(block_e, block_valid, x_rows, wg, wu, wd)


def _combine_kernel(d0_ref, d1_ref, x2_ref, gc_ref, y_ref, o_ref, y0_ref, y1_ref, sem):
    tm = x2_ref.shape[0]

    def issue(g, c):
        for i in range(MOVE_UNROLL):
            t = g * MOVE_UNROLL + i
            _row_copy(y_ref, d0_ref[t], y0_ref, t, sem).start()
            _row_copy(y_ref, d1_ref[t], y1_ref, t, sem).start()
        return c

    lax.fori_loop(0, tm // MOVE_UNROLL, issue, 0)
    pltpu.make_async_copy(y_ref.at[pl.ds(0, tm), :], y0_ref, sem).wait()
    pltpu.make_async_copy(y_ref.at[pl.ds(0, tm), :], y1_ref, sem).wait()
    gc = gc_ref[...]
    o_ref[...] = x2_ref[...] + (gc[:, 0:1] * y0_ref[...] + gc[:, 1:2] * y1_ref[...])


def _combine_call(dest0, dest1, x2, gc, y_rows):
    n = x2.shape[0]
    tm = TM_MOVE
    idx = pl.BlockSpec((tm,), lambda i: (i,), memory_space=pltpu.SMEM)
    return pl.pallas_call(
        _combine_kernel,
        grid=(n // tm,),
        in_specs=[idx, idx, pl.BlockSpec((tm, D_MODEL), lambda i: (i, 0)), pl.BlockSpec((tm, LANES), lambda i: (i, 0)),
                  pl.BlockSpec(memory_space=pl.ANY)],
        out_specs=pl.BlockSpec((tm, D_MODEL), lambda i: (i, 0)),
        out_shape=jax.ShapeDtypeStruct((n, D_MODEL), F32),
        scratch_shapes=[pltpu.VMEM((tm, D_MODEL), F32), pltpu.VMEM((tm, D_MODEL), F32), pltpu.SemaphoreType.DMA],
        compiler_params=pltpu.CompilerParams(dimension_semantics=("arbitrary",), vmem_limit_bytes=VMEM_LIMIT,
                                             disable_bounds_checks=True),
        name="combine",
    )(dest0, dest1, x2, gc, y_rows)


def _prepare(norm_mix, w_in, a_v_norm, a_spatial_w, a_spatial_b, q_norm, k_norm, out_norm_a, out_norm_b, w_out,
             norm_ffn, w_router_group, b_router_group, w_router_expert, b_router_expert,
             w_expert_gate, w_expert_up, w_expert_down):
    ch = jnp.arange(A_WIDTH) // GROUP_DIM
    pad = ROUTER_ROWS - N_EXPERTS - N_GROUPS
    dist = jnp.abs(jnp.arange(TK)[None, :] - jnp.arange(TQ)[:, None] - HALF).astype(F32)
    slope = 2.0 ** (-8.0 * (jnp.arange(HEADS, dtype=F32) + 1.0) / HEADS)
    dil = jnp.asarray(DILATIONS, F32)
    abias = jnp.where(dist <= HALF, -slope[None, :, None, None] * (dist * dil[:, None, None, None]), NEG_INF)
    abias = abias.reshape(len(DILATIONS), B_WIDTH // LANES, 2, TQ, TK).transpose(1, 0, 2, 3, 4)
    return dict(
        gmix=norm_mix.reshape(1, D_MODEL),
        win=w_in.astype(BF16),
        bd=(ch[:, None] == ch[None, :]).astype(BF16),
        avn=a_v_norm.reshape(1, A_WIDTH),
        wcat=jnp.concatenate([a_spatial_w[0::2], a_spatial_w[1::2]], axis=2).astype(BF16),
        bias=jnp.repeat(a_spatial_b.T, GROUP_DIM, axis=1),
        gq=(jnp.tile(q_norm, HEADS) * (HEAD_DIM ** -0.5)).reshape(1, B_WIDTH),
        gk=jnp.tile(k_norm, HEADS).reshape(1, B_WIDTH),
        gna=out_norm_a.reshape(1, A_WIDTH),
        gnb=out_norm_b.reshape(1, B_WIDTH),
        abias=abias,
        wout=w_out.astype(BF16),
        gffn=norm_ffn.reshape(1, D_MODEL),
        wr=jnp.concatenate([w_router_expert.T, w_router_group.T, jnp.zeros((pad, D_MODEL), F32)], axis=0).astype(BF16),
        br=jnp.concatenate([b_router_expert, b_router_group, jnp.zeros((pad,), F32)]).reshape(ROUTER_ROWS, 1),
        tri=(jnp.arange(TM_OUT)[:, None] < jnp.arange(TM_OUT)[None, :]).astype(BF16),
        wg=w_expert_gate.astype(BF16),
        wu=w_expert_up.astype(BF16),
        wd=w_expert_down.astype(BF16),
    )


def _layer(x, p):
    b, t, _ = x.shape
    n = b * t
    x2d = x.reshape(n, D_MODEL)
    ya, q, k, v = _proj_call(x2d, p["gmix"], p["win"], p["bd"], p["avn"], p["wcat"], p["bias"], p["gq"], p["gk"], p["gna"])
    q, k, v = (a.reshape(b, t, B_WIDTH) for a in (q, k, v))
    att = _attn_call(q, k, v, p["abias"]).reshape(n, B_WIDTH)
    x2, ei, gc, cnt = _out_call(x2d, ya, att, p["gnb"], p["wout"], p["gffn"], p["wr"], p["br"], p["tri"])

    counts = cnt[:, 0].astype(jnp.int32)
    pcounts = (counts + ROW_BLOCK - 1) // ROW_BLOCK * ROW_BLOCK
    pends = jnp.cumsum(pcounts)
    pstarts = pends - pcounts
    nb = (2 * n) // ROW_BLOCK + N_EXPERTS
    starts = jnp.arange(nb, dtype=jnp.int32) * ROW_BLOCK
    block_e = jnp.minimum(jnp.sum((pends[None, :] <= starts[:, None]).astype(jnp.int32), axis=1), N_EXPERTS - 1)
    block_valid = jnp.clip(pstarts[block_e] + counts[block_e] - starts, 0, ROW_BLOCK)
    dest0 = pstarts[ei[0]] + ei[2]
    dest1 = pstarts[ei[1]] + ei[3]
    used = pends[N_EXPERTS - 1] // ROW_BLOCK
    tail = used + jnp.arange(N_EXPERTS, dtype=jnp.int32)
    zero_blocks = jnp.concatenate([jnp.maximum(pends // ROW_BLOCK - 1, 0), jnp.minimum(tail, nb - 1)]).astype(jnp.int32)
    zero_on = jnp.concatenate([pcounts > 0, tail < nb]).astype(jnp.int32)

    x_rows = _dispatch_call(zero_blocks, zero_on, dest0, dest1, x2, p["gffn"], nb * ROW_BLOCK)
    y_rows = _expert_call(block_e, block_valid, x_rows, p["wg"], p["wu"], p["wd"])
    out = _combine_call(dest0, dest1, x2, gc, y_rows)
    return out.reshape(b, t, D_MODEL)


def kernel(x_prompt, x_sample, norm_mix, w_in, a_v_norm, a_spatial_w, a_spatial_b, q_norm, k_norm, out_norm_a,
           out_norm_b, w_out, norm_ffn, w_router_group, b_router_group, w_router_expert, b_router_expert,
           w_expert_gate, w_expert_up, w_expert_down):
    depth = norm_mix.shape[0]
    layers = [
        _prepare(norm_mix[l], w_in[l], a_v_norm[l], a_spatial_w[l], a_spatial_b[l], q_norm[l], k_norm[l],
                 out_norm_a[l], out_norm_b[l], w_out[l], norm_ffn[l], w_router_group[l], b_router_group[l],
                 w_router_expert[l], b_router_expert[l], w_expert_gate[l], w_expert_up[l], w_expert_down[l])
        for l in range(depth)
    ]

    def run(x):
        for p in layers:
            x = _layer(x, p)
        return x

    return (run(x_prompt), run(x_sample))
```

```python
import jax
import jax.numpy as jnp
from jax import lax
from jax.experimental import pallas as pl
from jax.experimental.pallas import tpu as pltpu

D_MODEL = 1024
A_WIDTH = 512
B_WIDTH = 512
IN_WIDTH = 2 * A_WIDTH + 3 * B_WIDTH
A_GROUPS = 8
GROUP_DIM = 64
CHUNK = 128
HEADS = 8
HEAD_DIM = 64
DILATIONS = (1, 4, 16)
HALF = 64
N_GROUPS = 4
EXPERTS_PER_GROUP = 8
N_EXPERTS = 32
D_EXPERT = 512
ROW_BLOCK = 512
EPS = 1e-6
NEG_INF = -1e30

LANES = 128
ROUTER_ROWS = 48
TM_PROJ = 512
TM_OUT = 512
TM_MOVE = 512
MOVE_UNROLL = 8
TQ = 128
TK = TQ + 2 * HALF
SUPER = TQ * max(DILATIONS)
HALO = HALF * max(DILATIONS)
VMEM_LIMIT = 48 * 1024 * 1024

F32 = jnp.float32
BF16 = jnp.bfloat16
NT_DIMS = (((1,), (1,)), ((), ()))


def _rms(x, gain):
    return x * lax.rsqrt(jnp.mean(x * x, axis=-1, keepdims=True) + EPS) * gain


def _proj_kernel(x_ref, gmix_ref, win_ref, bd_ref, avn_ref, wcat_ref, bias_ref, gq_ref, gk_ref, gna_ref,
                 ya_ref, q_ref, k_ref, v_ref):
    h = _rms(x_ref[...], gmix_ref[...])
    proj = jnp.dot(h.astype(BF16), win_ref[...], preferred_element_type=F32)
    pu = proj[:, 0:A_WIDTH]
    pv = proj[:, A_WIDTH:2 * A_WIDTH]
    q = proj[:, 2 * A_WIDTH:2 * A_WIDTH + B_WIDTH]
    k = proj[:, 2 * A_WIDTH + B_WIDTH:2 * A_WIDTH + 2 * B_WIDTH]
    v = proj[:, 2 * A_WIDTH + 2 * B_WIDTH:]
    bd = bd_ref[...]

    def group_norm(t, gain):
        ms = jnp.dot((t * t).astype(BF16), bd, preferred_element_type=F32) * (1.0 / GROUP_DIM)
        return t * lax.rsqrt(ms + EPS) * gain

    u = jax.nn.gelu(pu)
    vn = group_norm(jax.nn.gelu(pv), avn_ref[...]).astype(BF16)
    lane = lax.broadcasted_iota(jnp.int32, (CHUNK, LANES), 1)
    lo = lane < GROUP_DIM
    zero = jnp.zeros((CHUNK, LANES), BF16)
    chunks = []
    for c in range(x_ref.shape[0] // CHUNK):
        blks = []
        for j in range(A_WIDTH // LANES):
            vb = vn[c * CHUNK:(c + 1) * CHUNK, j * LANES:(j + 1) * LANES]
            rhs = jnp.concatenate([jnp.where(lo, vb, zero), jnp.where(lo, zero, vb)], axis=0)
            blks.append(jnp.dot(wcat_ref[j], rhs, preferred_element_type=F32))
        chunks.append(jnp.concatenate(blks, axis=1) + bias_ref[...])
    mixed = jnp.concatenate(chunks, axis=0)
    ya_ref[...] = _rms(u * mixed, gna_ref[...]).astype(BF16)
    q_ref[...] = group_norm(q, gq_ref[...]).astype(BF16)
    k_ref[...] = group_norm(k, gk_ref[...]).astype(BF16)
    v_ref[...] = v.astype(BF16)


def _proj_call(x2d, gmix, win, bd, avn, wcat, bias, gq, gk, gna):
    n = x2d.shape[0]
    tm = TM_PROJ
    full = lambda shape: pl.BlockSpec(shape, lambda i: (0,) * len(shape))
    tok = lambda w: pl.BlockSpec((tm, w), lambda i: (i, 0))
    return pl.pallas_call(
        _proj_kernel,
        grid=(n // tm,),
        in_specs=[tok(D_MODEL), full((1, D_MODEL)), full((D_MODEL, IN_WIDTH)), full((A_WIDTH, A_WIDTH)),
                  full((1, A_WIDTH)), full((A_WIDTH // LANES, CHUNK, 2 * CHUNK)), full((CHUNK, A_WIDTH)),
                  full((1, B_WIDTH)), full((1, B_WIDTH)), full((1, A_WIDTH))],
        out_specs=[tok(A_WIDTH), tok(B_WIDTH), tok(B_WIDTH), tok(B_WIDTH)],
        out_shape=[jax.ShapeDtypeStruct((n, A_WIDTH), BF16)] + [jax.ShapeDtypeStruct((n, B_WIDTH), BF16)] * 3,
        compiler_params=pltpu.CompilerParams(dimension_semantics=("parallel",), vmem_limit_bytes=VMEM_LIMIT),
        name="proj_gating",
    )(x2d, gmix, win, bd, avn, wcat, bias, gq, gk, gna)


def _attn_kernel(bias_ref, q_ref, kp_ref, kc_ref, kn_ref, vp_ref, vc_ref, vn_ref, o_ref,
                 qf, kf, vf, acc16, m16, l16, acc4, m4, l4):
    sb = pl.program_id(1)

    qf[...] = q_ref[...].astype(F32)
    kf[0:HALO, :] = kp_ref[...].astype(F32)
    kf[HALO:HALO + SUPER, :] = kc_ref[...].astype(F32)
    kf[HALO + SUPER:, :] = kn_ref[...].astype(F32)
    vf[0:HALO, :] = vp_ref[...].astype(F32)
    vf[HALO:HALO + SUPER, :] = vc_ref[...].astype(F32)
    vf[HALO + SUPER:, :] = vn_ref[...].astype(F32)

    lane = lax.broadcasted_iota(jnp.int32, (TQ, LANES), 1)
    lo = lane < HEAD_DIM

    def branch_unit(dil, res, qs):
        n_sub = SUPER // dil
        q_start = dil * qs + res
        k_start = HALO + dil * (qs - HALF) + res
        if dil == 1:
            qb = qf[pl.ds(q_start, TQ), :]
            kb = kf[pl.ds(k_start, TK), :]
            vb = vf[pl.ds(k_start, TK), :]
        else:
            qb = qf[pl.ds(q_start, TQ, stride=dil), :]
            kb = kf[pl.ds(k_start, TK, stride=dil), :]
            vb = vf[pl.ds(k_start, TK, stride=dil), :]
        qb = qb.astype(BF16)
        kb = kb.astype(BF16)
        vb = vb.astype(BF16)
        branch = DILATIONS.index(dil)
        edge = qs == 0 or qs + TQ == n_sub
        if edge:
            col = lax.broadcasted_iota(jnp.int32, (1, TK), 1)
            c_lo = jnp.where(sb == 0, HALF - qs, 0)
            c_hi = jnp.where(sb == pl.num_programs(1) - 1, n_sub - qs + HALF, TK)
            in_seq = (col >= c_lo) & (col < c_hi)
        res_h = []
        for hh in range(2):
            qm = jnp.where(lo if hh == 0 else ~lo, qb, jnp.zeros_like(qb))
            s = lax.dot_general(qm, kb, NT_DIMS, preferred_element_type=F32)
            s = s + bias_ref[branch, hh]
            if edge:
                s = jnp.where(in_seq, s, NEG_INF)
            m = jnp.max(s, axis=-1, keepdims=True)
            p = jnp.exp(s - m)
            l = jnp.sum(p, axis=-1, keepdims=True)
            pv = jnp.dot(p.astype(BF16), vb, preferred_element_type=F32)
            res_h.append((pv, m, l))
        return tuple(jnp.where(lo, a, b) for a, b in zip(res_h[0], res_h[1]))

    def strided_branch(dil, acc_ref, m_ref, l_ref):
        for u in range(SUPER // TQ):
            res = u % dil
            qs = (u // dil) * TQ
            acc, m, l = branch_unit(dil, res, qs)
            rows = pl.ds(dil * qs + res, TQ, stride=dil)
            acc_ref[rows, :] = acc
            m_ref[rows, :] = m
            l_ref[rows, :] = l

    strided_branch(16, acc16, m16, l16)
    strided_branch(4, acc4, m4, l4)

    for u in range(SUPER // TQ):
        qs = u * TQ
        a1, m1, l1 = branch_unit(1, 0, qs)
        rows = pl.ds(qs, TQ)
        m_4, m_16 = m4[rows, :], m16[rows, :]
        m_all = jnp.maximum(jnp.maximum(m1, m_4), m_16)
        e1 = jnp.exp(m1 - m_all)
        e4 = jnp.exp(m_4 - m_all)
        e16 = jnp.exp(m_16 - m_all)
        num = e1 * a1 + e4 * acc4[rows, :] + e16 * acc16[rows, :]
        den = e1 * l1 + e4 * l4[rows, :] + e16 * l16[rows, :]
        o_ref[rows, :] = num / den


def _attn_call(q, k, v, bias):
    b, t, _ = q.shape
    nsb = t // SUPER
    nkb = t // HALO
    per = SUPER // HALO
    cur = lambda bb, s, j: (bb, s, j)
    prv = lambda bb, s, j: (bb, jnp.maximum(s * per - 1, 0), j)
    nxt = lambda bb, s, j: (bb, jnp.minimum((s + 1) * per, nkb - 1), j)
    blk = lambda rows, f: pl.BlockSpec((None, rows, LANES), f)
    bias_spec = pl.BlockSpec((None,) + bias.shape[1:], lambda bb, s, j: (j, 0, 0, 0, 0))
    stat = pltpu.VMEM((SUPER, LANES), F32)
    kv = pltpu.VMEM((SUPER + 2 * HALO, LANES), F32)
    return pl.pallas_call(
        _attn_kernel,
        grid=(b, nsb, B_WIDTH // LANES),
        in_specs=[bias_spec, blk(SUPER, cur), blk(HALO, prv), blk(SUPER, cur), blk(HALO, nxt),
                  blk(HALO, prv), blk(SUPER, cur), blk(HALO, nxt)],
        out_specs=blk(SUPER, cur),
        out_shape=jax.ShapeDtypeStruct((b, t, B_WIDTH), F32),
        scratch_shapes=[stat, kv, kv, stat, stat, stat, stat, stat, stat],
        compiler_params=pltpu.CompilerParams(dimension_semantics=("parallel", "parallel", "parallel"),
                                             vmem_limit_bytes=VMEM_LIMIT),
        name="attention",
    )(bias, q, k, k, k, v, v, v)


def _out_kernel(x_ref, ya_ref, att_ref, gnb_ref, wout_ref, gffn_ref, wr_ref, br_ref, tri_ref,
                x2_ref, ei_ref, gc_ref, cnt_ref, base_ref):
    tm = x_ref.shape[0]

    @pl.when(pl.program_id(0) == 0)
    def _():
        base_ref[...] = jnp.zeros_like(base_ref)

    yb = _rms(att_ref[...], gnb_ref[...]).astype(BF16)
    a = jnp.concatenate([ya_ref[...], yb], axis=1)
    x2 = x_ref[...] + jnp.dot(a, wout_ref[...], preferred_element_type=F32)
    x2_ref[...] = x2
    xn = _rms(x2, gffn_ref[...])

    lg = lax.dot_general(wr_ref[...], xn.astype(BF16), NT_DIMS, preferred_element_type=F32) + br_ref[...]
    e_log = lg[0:N_EXPERTS]
    g_log = lg[N_EXPERTS:N_EXPERTS + N_GROUPS]
    r4 = lax.broadcasted_iota(jnp.int32, (N_GROUPS, tm), 0).astype(F32)
    g_max = jnp.max(g_log, axis=0, keepdims=True)
    g_sel = jnp.min(jnp.where(g_log == g_max, r4, float(N_GROUPS)), axis=0, keepdims=True)
    p_group = 1.0 / jnp.sum(jnp.exp(g_log - g_max), axis=0, keepdims=True)
    e_sel = jnp.zeros((EXPERTS_PER_GROUP, tm), F32)
    for g in range(N_GROUPS):
        e_sel = jnp.where(g_sel == float(g), e_log[g * EXPERTS_PER_GROUP:(g + 1) * EXPERTS_PER_GROUP], e_sel)
    r8 = lax.broadcasted_iota(jnp.int32, (EXPERTS_PER_GROUP, tm), 0).astype(F32)
    v1 = jnp.max(e_sel, axis=0, keepdims=True)
    i1 = jnp.min(jnp.where(e_sel == v1, r8, float(EXPERTS_PER_GROUP)), axis=0, keepdims=True)
    e_rest = jnp.where(r8 == i1, -jnp.inf, e_sel)
    v2 = jnp.max(e_rest, axis=0, keepdims=True)
    i2 = jnp.min(jnp.where(e_rest == v2, r8, float(EXPERTS_PER_GROUP)), axis=0, keepdims=True)
    d = jnp.exp(v2 - v1)
    gate1 = p_group * (1.0 / (1.0 + d))
    gate2 = p_group * (d / (1.0 + d))
    eid1 = g_sel * float(EXPERTS_PER_GROUP) + i1
    eid2 = g_sel * float(EXPERTS_PER_GROUP) + i2

    r32 = lax.broadcasted_iota(jnp.int32, (N_EXPERTS, tm), 0).astype(F32)
    oh1 = r32 == eid1
    oh2 = r32 == eid2
    oh1f = jnp.where(oh1, 1.0, 0.0)
    oh2f = jnp.where(oh2, 1.0, 0.0)
    tri = tri_ref[...]
    pre1 = jnp.dot(oh1f.astype(BF16), tri, preferred_element_type=F32)
    pre2 = jnp.dot(oh2f.astype(BF16), tri, preferred_element_type=F32)
    tot1 = jnp.sum(oh1f, axis=1, keepdims=True)
    tot2 = jnp.sum(oh2f, axis=1, keepdims=True)
    base_full = base_ref[...]
    base = base_full[:, 0:1]
    rank1 = jnp.sum(jnp.where(oh1, base + pre1, 0.0), axis=0, keepdims=True)
    rank2 = jnp.sum(jnp.where(oh2, base + tot1 + pre2, 0.0), axis=0, keepdims=True)
    base_full = base_full + tot1 + tot2
    base_ref[...] = base_full
    cnt_ref[...] = base_full
    ei_ref[...] = jnp.concatenate([eid1, eid2, rank1, rank2], axis=0).astype(jnp.int32)
    r128 = lax.broadcasted_iota(jnp.int32, (LANES, tm), 0)
    gates_rows = jnp.where(r128 == 0, gate1, jnp.where(r128 == 1, gate2, 0.0))
    gc_ref[...] = gates_rows.T


def _out_call(x2d, ya, att, gnb, wout, gffn, wr, br, tri):
    n = x2d.shape[0]
    tm = TM_OUT
    full = lambda shape: pl.BlockSpec(shape, lambda i: (0,) * len(shape))
    tok = lambda w: pl.BlockSpec((tm, w), lambda i: (i, 0))
    return pl.pallas_call(
        _out_kernel,
        grid=(n // tm,),
        in_specs=[tok(D_MODEL), tok(A_WIDTH), tok(B_WIDTH), full((1, B_WIDTH)), full((D_MODEL, D_MODEL)),
                  full((1, D_MODEL)), full((ROUTER_ROWS, D_MODEL)), full((ROUTER_ROWS, 1)), full((tm, tm))],
        out_specs=[tok(D_MODEL), pl.BlockSpec((4, tm), lambda i: (0, i)), tok(LANES),
                   full((N_EXPERTS, LANES))],
        out_shape=[jax.ShapeDtypeStruct((n, D_MODEL), F32),
                   jax.ShapeDtypeStruct((4, n), jnp.int32), jax.ShapeDtypeStruct((n, LANES), F32),
                   jax.ShapeDtypeStruct((N_EXPERTS, LANES), F32)],
        scratch_shapes=[pltpu.VMEM((N_EXPERTS, LANES), F32)],
        compiler_params=pltpu.CompilerParams(dimension_semantics=("arbitrary",), vmem_limit_bytes=VMEM_LIMIT),
        name="out_router",
    )(x2d, ya, att, gnb, wout, gffn, wr, br, tri)


def _row_copy(src, s, dst, d, sem):
    return pltpu.make_async_copy(src.at[pl.ds(s, 1), :], dst.at[pl.ds(d, 1), :], sem)


def _dispatch_kernel(zb_ref, zon_ref, d0_ref, d1_ref, x2_ref, gffn_ref, rows_ref, xn_buf, zero_ref, sems, zsem):
    tm = x2_ref.shape[0]
    step = pl.program_id(0)
    slot = step % 2
    xn_ref = xn_buf.at[slot]
    sem = sems.at[slot]
    xn_ref[...] = _rms(x2_ref[...], gffn_ref[...])

    @pl.when(pl.program_id(0) == 0)
    def _():
        zero_ref[...] = jnp.zeros_like(zero_ref)

        def zero_copy(i):
            start = pl.multiple_of(zb_ref[i] * ROW_BLOCK, ROW_BLOCK)
            return pltpu.make_async_copy(zero_ref, rows_ref.at[pl.ds(start, ROW_BLOCK), :], zsem)

        def start(i, c):
            @pl.when(zon_ref[i] == 1)
            def _():
                zero_copy(i).start()
            return c

        def wait(i, c):
            @pl.when(zon_ref[i] == 1)
            def _():
                zero_copy(i).wait()
            return c

        lax.fori_loop(0, 2 * N_EXPERTS, start, 0)
        lax.fori_loop(0, 2 * N_EXPERTS, wait, 0)

    def issue(g, c):
        for i in range(MOVE_UNROLL):
            t = g * MOVE_UNROLL + i
            _row_copy(xn_ref, t, rows_ref, d0_ref[t], sem).start()
            _row_copy(xn_ref, t, rows_ref, d1_ref[t], sem).start()
        return c

    lax.fori_loop(0, tm // MOVE_UNROLL, issue, 0)

    def wait_tile(s):
        tile = pltpu.make_async_copy(xn_buf.at[s], rows_ref.at[pl.ds(0, tm), :], sems.at[s])
        tile.wait()
        tile.wait()

    @pl.when(step > 0)
    def _():
        wait_tile(1 - slot)

    @pl.when(step == pl.num_programs(0) - 1)
    def _():
        wait_tile(slot)


def _dispatch_call(zero_blocks, zero_on, dest0, dest1, x2, gffn, n_rows):
    n = x2.shape[0]
    tm = TM_MOVE
    idx = pl.BlockSpec((tm,), lambda i, zb, zon: (i,), memory_space=pltpu.SMEM)
    grid_spec = pltpu.PrefetchScalarGridSpec(
        num_scalar_prefetch=2,
        grid=(n // tm,),
        in_specs=[idx, idx, pl.BlockSpec((tm, D_MODEL), lambda i, zb, zon: (i, 0)),
                  pl.BlockSpec((1, D_MODEL), lambda i, zb, zon: (0, 0))],
        out_specs=pl.BlockSpec(memory_space=pl.ANY),
        scratch_shapes=[pltpu.VMEM((2, tm, D_MODEL), F32), pltpu.VMEM((ROW_BLOCK, D_MODEL), F32),
                        pltpu.SemaphoreType.DMA((2,)), pltpu.SemaphoreType.DMA],
    )
    return pl.pallas_call(
        _dispatch_kernel,
        grid_spec=grid_spec,
        out_shape=jax.ShapeDtypeStruct((n_rows, D_MODEL), F32),
        compiler_params=pltpu.CompilerParams(dimension_semantics=("arbitrary",), has_side_effects=True,
                                             disable_bounds_checks=True),
        name="dispatch",
    )(zero_blocks, zero_on, dest0, dest1, x2, gffn)


def _expert_kernel(be_ref, bv_ref, x_ref, wg_ref, wu_ref, wd_ref, y_ref):
    del be_ref
    valid = bv_ref[pl.program_id(0)]

    @pl.when(valid > 0)
    def _():
        xb = x_ref[...].astype(BF16)
        g = jnp.dot(xb, wg_ref[...], preferred_element_type=F32)
        u = jnp.dot(xb, wu_ref[...], preferred_element_type=F32)
        h = (jax.nn.silu(g) * u).astype(BF16)
        y_ref[...] = jnp.dot(h, wd_ref[...], preferred_element_type=F32)

    @pl.when(valid == 0)
    def _():
        y_ref[...] = jnp.zeros_like(y_ref)


def _expert_call(block_e, block_valid, x_rows, wg, wu, wd):
    n_rows = x_rows.shape[0]
    nb = n_rows // ROW_BLOCK
    grid_spec = pltpu.PrefetchScalarGridSpec(
        num_scalar_prefetch=2,
        grid=(nb,),
        in_specs=[pl.BlockSpec((ROW_BLOCK, D_MODEL), lambda b, be, bv: (b, 0)),
                  pl.BlockSpec((None, D_MODEL, D_EXPERT), lambda b, be, bv: (be[b], 0, 0)),
                  pl.BlockSpec((None, D_MODEL, D_EXPERT), lambda b, be, bv: (be[b], 0, 0)),
                  pl.BlockSpec((None, D_EXPERT, D_MODEL), lambda b, be, bv: (be[b], 0, 0))],
        out_specs=pl.BlockSpec((ROW_BLOCK, D_MODEL), lambda b, be, bv: (b, 0)),
    )
    return pl.pallas_call(
        _expert_kernel,
        grid_spec=grid_spec,
        out_shape=jax.ShapeDtypeStruct((n_rows, D_MODEL), F32),
        compiler_params=pltpu.CompilerParams(dimension_semantics=("arbitrary",), vmem_limit_bytes=VMEM_LIMIT),
        name="experts",
    )(block_e, block_valid, x_rows, wg, wu, wd)


def _combine_kernel(d0_ref, d1_ref, n0_ref, n1_ref, x2_ref, gc_ref, y_ref, o_ref, y0_buf, y1_buf, sems):
    tm = x2_ref.shape[0]
    step = pl.program_id(0)
    slot = step % 2

    def gather(i0_ref, i1_ref, s):
        def issue(g, c):
            for i in range(MOVE_UNROLL):
                t = g * MOVE_UNROLL + i
                _row_copy(y_ref, i0_ref[t], y0_buf.at[s], t, sems.at[s]).start()
                _row_copy(y_ref, i1_ref[t], y1_buf.at[s], t, sems.at[s]).start()
            return c

        lax.fori_loop(0, tm // MOVE_UNROLL, issue, 0)

    @pl.when(step == 0)
    def _():
        gather(d0_ref, d1_ref, slot)

    @pl.when(step < pl.num_programs(0) - 1)
    def _():
        gather(n0_ref, n1_ref, 1 - slot)

    pltpu.make_async_copy(y_ref.at[pl.ds(0, tm), :], y0_buf.at[slot], sems.at[slot]).wait()
    pltpu.make_async_copy(y_ref.at[pl.ds(0, tm), :], y1_buf.at[slot], sems.at[slot]).wait()
    gc = gc_ref[...]
    o_ref[...] = x2_ref[...] + (gc[:, 0:1] * y0_buf[slot] + gc[:, 1:2] * y1_buf[slot])


def _combine_call(dest0, dest1, x2, gc, y_rows):
    n = x2.shape[0]
    tm = TM_MOVE
    last = n // tm - 1
    idx = pl.BlockSpec((tm,), lambda i: (i,), memory_space=pltpu.SMEM)
    idx_next = pl.BlockSpec((tm,), lambda i: (jnp.minimum(i + 1, last),), memory_space=pltpu.SMEM)
    return pl.pallas_call(
        _combine_kernel,
        grid=(n // tm,),
        in_specs=[idx, idx, idx_next, idx_next, pl.BlockSpec((tm, D_MODEL), lambda i: (i, 0)),
                  pl.BlockSpec((tm, LANES), lambda i: (i, 0)), pl.BlockSpec(memory_space=pl.ANY)],
        out_specs=pl.BlockSpec((tm, D_MODEL), lambda i: (i, 0)),
        out_shape=jax.ShapeDtypeStruct((n, D_MODEL), F32),
        scratch_shapes=[pltpu.VMEM((2, tm, D_MODEL), F32), pltpu.VMEM((2, tm, D_MODEL), F32),
                        pltpu.SemaphoreType.DMA((2,))],
        compiler_params=pltpu.CompilerParams(dimension_semantics=("arbitrary",), vmem_limit_bytes=VMEM_LIMIT,
                                             disable_bounds_checks=True),
        name="combine",
    )(dest0, dest1, dest0, dest1, x2, gc, y_rows)


def _prepare(norm_mix, w_in, a_v_norm, a_spatial_w, a_spatial_b, q_norm, k_norm, out_norm_a, out_norm_b, w_out,
             norm_ffn, w_router_group, b_router_group, w_router_expert, b_router_expert,
             w_expert_gate, w_expert_up, w_expert_down):
    ch = jnp.arange(A_WIDTH) // GROUP_DIM
    pad = ROUTER_ROWS - N_EXPERTS - N_GROUPS
    dist = jnp.abs(jnp.arange(TK)[None, :] - jnp.arange(TQ)[:, None] - HALF).astype(F32)
    slope = 2.0 ** (-8.0 * (jnp.arange(HEADS, dtype=F32) + 1.0) / HEADS)
    dil = jnp.asarray(DILATIONS, F32)
    abias = jnp.where(dist <= HALF, -slope[None, :, None, None] * (dist * dil[:, None, None, None]), NEG_INF)
    abias = abias.reshape(len(DILATIONS), B_WIDTH // LANES, 2, TQ, TK).transpose(1, 0, 2, 3, 4)
    return dict(
        gmix=norm_mix.reshape(1, D_MODEL),
        win=w_in.astype(BF16),
        bd=(ch[:, None] == ch[None, :]).astype(BF16),
        avn=a_v_norm.reshape(1, A_WIDTH),
        wcat=jnp.concatenate([a_spatial_w[0::2], a_spatial_w[1::2]], axis=2).astype(BF16),
        bias=jnp.repeat(a_spatial_b.T, GROUP_DIM, axis=1),
        gq=(jnp.tile(q_norm, HEADS) * (HEAD_DIM ** -0.5)).reshape(1, B_WIDTH),
        gk=jnp.tile(k_norm, HEADS).reshape(1, B_WIDTH),
        gna=out_norm_a.reshape(1, A_WIDTH),
        gnb=out_norm_b.reshape(1, B_WIDTH),
        abias=abias,
        wout=w_out.astype(BF16),
        gffn=norm_ffn.reshape(1, D_MODEL),
        wr=jnp.concatenate([w_router_expert.T, w_router_group.T, jnp.zeros((pad, D_MODEL), F32)], axis=0).astype(BF16),
        br=jnp.concatenate([b_router_expert, b_router_group, jnp.zeros((pad,), F32)]).reshape(ROUTER_ROWS, 1),
        tri=(jnp.arange(TM_OUT)[:, None] < jnp.arange(TM_OUT)[None, :]).astype(BF16),
        wg=w_expert_gate.astype(BF16),
        wu=w_expert_up.astype(BF16),
        wd=w_expert_down.astype(BF16),
    )


def _layer(x, p):
    b, t, _ = x.shape
    n = b * t
    x2d = x.reshape(n, D_MODEL)
    ya, q, k, v = _proj_call(x2d, p["gmix"], p["win"], p["bd"], p["avn"], p["wcat"], p["bias"], p["gq"], p["gk"], p["gna"])
    q, k, v = (a.reshape(b, t, B_WIDTH) for a in (q, k, v))
    att = _attn_call(q, k, v, p["abias"]).reshape(n, B_WIDTH)
    x2, ei, gc, cnt = _out_call(x2d, ya, att, p["gnb"], p["wout"], p["gffn"], p["wr"], p["br"], p["tri"])

    counts = cnt[:, 0].astype(jnp.int32)
    pcounts = (counts + ROW_BLOCK - 1) // ROW_BLOCK * ROW_BLOCK
    pends = jnp.cumsum(pcounts)
    pstarts = pends - pcounts
    nb = (2 * n) // ROW_BLOCK + N_EXPERTS
    starts = jnp.arange(nb, dtype=jnp.int32) * ROW_BLOCK
    block_e = jnp.minimum(jnp.sum((pends[None, :] <= starts[:, None]).astype(jnp.int32), axis=1), N_EXPERTS - 1)
    block_valid = jnp.clip(pstarts[block_e] + counts[block_e] - starts, 0, ROW_BLOCK)
    dest0 = pstarts[ei[0]] + ei[2]
    dest1 = pstarts[ei[1]] + ei[3]
    used = pends[N_EXPERTS - 1] // ROW_BLOCK
    tail = used + jnp.arange(N_EXPERTS, dtype=jnp.int32)
    zero_blocks = jnp.concatenate([jnp.maximum(pends // ROW_BLOCK - 1, 0), jnp.minimum(tail, nb - 1)]).astype(jnp.int32)
    zero_on = jnp.concatenate([pcounts > 0, tail < nb]).astype(jnp.int32)

    x_rows = _dispatch_call(zero_blocks, zero_on, dest0, dest1, x2, p["gffn"], nb * ROW_BLOCK)
    y_rows = _expert_call(block_e, block_valid, x_rows, p["wg"], p["wu"], p["wd"])
    out = _combine_call(dest0, dest1, x2, gc, y_rows)
    return out.reshape(b, t, D_MODEL)


def kernel(x_prompt, x_sample, norm_mix, w_in, a_v_norm, a_spatial_w, a_spatial_b, q_norm, k_norm, out_norm_a,
           out_norm_b, w_out, norm_ffn, w_router_group, b_router_group, w_router_expert, b_router_expert,
           w_expert_gate, w_expert_up, w_expert_down):
    depth = norm_mix.shape[0]
    layers = [
        _prepare(norm_mix[l], w_in[l], a_v_norm[l], a_spatial_w[l], a_spatial_b[l], q_norm[l], k_norm[l],
                 out_norm_a[l], out_norm_b[l], w_out[l], norm_ffn[l], w_router_group[l], b_router_group[l],
                 w_router_expert[l], b_router_expert[l], w_expert_gate[l], w_expert_up[l], w_expert_down[l])
        for l in range(depth)
    ]

    def run(x):
        for p in layers:
            x = _layer(x, p)
        return x

    return (run(x_prompt), run(x_sample))
```

```python
import jax
import jax.numpy as jnp
from jax import lax
from jax.experimental import pallas as pl
from jax.experimental.pallas import tpu as pltpu

D_MODEL = 1024
A_WIDTH = 512
B_WIDTH = 512
IN_WIDTH = 2 * A_WIDTH + 3 * B_WIDTH
A_GROUPS = 8
GROUP_DIM = 64
CHUNK = 128
HEADS = 8
HEAD_DIM = 64
DILATIONS = (1, 4, 16)
HALF = 64
N_GROUPS = 4
EXPERTS_PER_GROUP = 8
N_EXPERTS = 32
D_EXPERT = 512
ROW_BLOCK = 512
EPS = 1e-6
NEG_INF = -1e30

LANES = 128
ROUTER_ROWS = 48
TM_PROJ = 512
TM_OUT = 512
TM_MOVE = 512
MOVE_UNROLL = 512
TQ = 128
TK = TQ + 2 * HALF
SUPER = TQ * max(DILATIONS)
HALO = HALF * max(DILATIONS)
VMEM_LIMIT = 48 * 1024 * 1024

F32 = jnp.float32
BF16 = jnp.bfloat16
NT_DIMS = (((1,), (1,)), ((), ()))


def _rms(x, gain):
    return x * lax.rsqrt(jnp.mean(x * x, axis=-1, keepdims=True) + EPS) * gain


def _proj_kernel(x_ref, gmix_ref, win_ref, bd_ref, avn_ref, wcat_ref, bias_ref, gq_ref, gk_ref, gna_ref,
                 ya_ref, q_ref, k_ref, v_ref):
    h = _rms(x_ref[...], gmix_ref[...])
    proj = jnp.dot(h.astype(BF16), win_ref[...], preferred_element_type=F32)
    pu = proj[:, 0:A_WIDTH]
    pv = proj[:, A_WIDTH:2 * A_WIDTH]
    q = proj[:, 2 * A_WIDTH:2 * A_WIDTH + B_WIDTH]
    k = proj[:, 2 * A_WIDTH + B_WIDTH:2 * A_WIDTH + 2 * B_WIDTH]
    v = proj[:, 2 * A_WIDTH + 2 * B_WIDTH:]
    bd = bd_ref[...]

    def group_norm(t, gain):
        ms = jnp.dot((t * t).astype(BF16), bd, preferred_element_type=F32) * (1.0 / GROUP_DIM)
        return t * lax.rsqrt(ms + EPS) * gain

    u = jax.nn.gelu(pu)
    vn = group_norm(jax.nn.gelu(pv), avn_ref[...]).astype(BF16)
    lane = lax.broadcasted_iota(jnp.int32, (CHUNK, LANES), 1)
    lo = lane < GROUP_DIM
    zero = jnp.zeros((CHUNK, LANES), BF16)
    chunks = []
    for c in range(x_ref.shape[0] // CHUNK):
        blks = []
        for j in range(A_WIDTH // LANES):
            vb = vn[c * CHUNK:(c + 1) * CHUNK, j * LANES:(j + 1) * LANES]
            rhs = jnp.concatenate([jnp.where(lo, vb, zero), jnp.where(lo, zero, vb)], axis=0)
            blks.append(jnp.dot(wcat_ref[j], rhs, preferred_element_type=F32))
        chunks.append(jnp.concatenate(blks, axis=1) + bias_ref[...])
    mixed = jnp.concatenate(chunks, axis=0)
    ya_ref[...] = _rms(u * mixed, gna_ref[...]).astype(BF16)
    q_ref[...] = group_norm(q, gq_ref[...]).astype(BF16)
    k_ref[...] = group_norm(k, gk_ref[...]).astype(BF16)
    v_ref[...] = v.astype(BF16)


def _proj_call(x2d, gmix, win, bd, avn, wcat, bias, gq, gk, gna):
    n = x2d.shape[0]
    tm = TM_PROJ
    full = lambda shape: pl.BlockSpec(shape, lambda i: (0,) * len(shape))
    tok = lambda w: pl.BlockSpec((tm, w), lambda i: (i, 0))
    return pl.pallas_call(
        _proj_kernel,
        grid=(n // tm,),
        in_specs=[tok(D_MODEL), full((1, D_MODEL)), full((D_MODEL, IN_WIDTH)), full((A_WIDTH, A_WIDTH)),
                  full((1, A_WIDTH)), full((A_WIDTH // LANES, CHUNK, 2 * CHUNK)), full((CHUNK, A_WIDTH)),
                  full((1, B_WIDTH)), full((1, B_WIDTH)), full((1, A_WIDTH))],
        out_specs=[tok(A_WIDTH), tok(B_WIDTH), tok(B_WIDTH), tok(B_WIDTH)],
        out_shape=[jax.ShapeDtypeStruct((n, A_WIDTH), BF16)] + [jax.ShapeDtypeStruct((n, B_WIDTH), BF16)] * 3,
        compiler_params=pltpu.CompilerParams(dimension_semantics=("parallel",), vmem_limit_bytes=VMEM_LIMIT),
        name="proj_gating",
    )(x2d, gmix, win, bd, avn, wcat, bias, gq, gk, gna)


def _attn_kernel(bias_ref, q_ref, kp_ref, kc_ref, kn_ref, vp_ref, vc_ref, vn_ref, o_ref,
                 qf, kf, vf, acc16, m16, l16, acc4, m4, l4):
    sb = pl.program_id(1)

    qf[...] = q_ref[...].astype(F32)
    kf[0:HALO, :] = kp_ref[...].astype(F32)
    kf[HALO:HALO + SUPER, :] = kc_ref[...].astype(F32)
    kf[HALO + SUPER:, :] = kn_ref[...].astype(F32)
    vf[0:HALO, :] = vp_ref[...].astype(F32)
    vf[HALO:HALO + SUPER, :] = vc_ref[...].astype(F32)
    vf[HALO + SUPER:, :] = vn_ref[...].astype(F32)

    lane = lax.broadcasted_iota(jnp.int32, (TQ, LANES), 1)
    lo = lane < HEAD_DIM

    def branch_unit(dil, res, qs):
        n_sub = SUPER // dil
        q_start = dil * qs + res
        k_start = HALO + dil * (qs - HALF) + res
        if dil == 1:
            qb = qf[pl.ds(q_start, TQ), :]
            kb = kf[pl.ds(k_start, TK), :]
            vb = vf[pl.ds(k_start, TK), :]
        else:
            qb = qf[pl.ds(q_start, TQ, stride=dil), :]
            kb = kf[pl.ds(k_start, TK, stride=dil), :]
            vb = vf[pl.ds(k_start, TK, stride=dil), :]
        qb = qb.astype(BF16)
        kb = kb.astype(BF16)
        vb = vb.astype(BF16)
        branch = DILATIONS.index(dil)
        edge = qs == 0 or qs + TQ == n_sub
        if edge:
            col = lax.broadcasted_iota(jnp.int32, (1, TK), 1)
            c_lo = jnp.where(sb == 0, HALF - qs, 0)
            c_hi = jnp.where(sb == pl.num_programs(1) - 1, n_sub - qs + HALF, TK)
            in_seq = (col >= c_lo) & (col < c_hi)
        res_h = []
        for hh in range(2):
            qm = jnp.where(lo if hh == 0 else ~lo, qb, jnp.zeros_like(qb))
            s = lax.dot_general(qm, kb, NT_DIMS, preferred_element_type=F32)
            s = s + bias_ref[branch, hh]
            if edge:
                s = jnp.where(in_seq, s, NEG_INF)
            m = jnp.max(s, axis=-1, keepdims=True)
            p = jnp.exp(s - m)
            l = jnp.sum(p, axis=-1, keepdims=True)
            pv = jnp.dot(p.astype(BF16), vb, preferred_element_type=F32)
            res_h.append((pv, m, l))
        return tuple(jnp.where(lo, a, b) for a, b in zip(res_h[0], res_h[1]))

    def strided_branch(dil, acc_ref, m_ref, l_ref):
        for u in range(SUPER // TQ):
            res = u % dil
            qs = (u // dil) * TQ
            acc, m, l = branch_unit(dil, res, qs)
            rows = pl.ds(dil * qs + res, TQ, stride=dil)
            acc_ref[rows, :] = acc
            m_ref[rows, :] = m
            l_ref[rows, :] = l

    strided_branch(16, acc16, m16, l16)
    strided_branch(4, acc4, m4, l4)

    for u in range(SUPER // TQ):
        qs = u * TQ
        a1, m1, l1 = branch_unit(1, 0, qs)
        rows = pl.ds(qs, TQ)
        m_4, m_16 = m4[rows, :], m16[rows, :]
        m_all = jnp.maximum(jnp.maximum(m1, m_4), m_16)
        e1 = jnp.exp(m1 - m_all)
        e4 = jnp.exp(m_4 - m_all)
        e16 = jnp.exp(m_16 - m_all)
        num = e1 * a1 + e4 * acc4[rows, :] + e16 * acc16[rows, :]
        den = e1 * l1 + e4 * l4[rows, :] + e16 * l16[rows, :]
        o_ref[rows, :] = num / den


def _attn_call(q, k, v, bias):
    b, t, _ = q.shape
    nsb = t // SUPER
    nkb = t // HALO
    per = SUPER // HALO
    cur = lambda bb, s, j: (bb, s, j)
    prv = lambda bb, s, j: (bb, jnp.maximum(s * per - 1, 0), j)
    nxt = lambda bb, s, j: (bb, jnp.minimum((s + 1) * per, nkb - 1), j)
    blk = lambda rows, f: pl.BlockSpec((None, rows, LANES), f)
    bias_spec = pl.BlockSpec((None,) + bias.shape[1:], lambda bb, s, j: (j, 0, 0, 0, 0))
    stat = pltpu.VMEM((SUPER, LANES), F32)
    kv = pltpu.VMEM((SUPER + 2 * HALO, LANES), F32)
    return pl.pallas_call(
        _attn_kernel,
        grid=(b, nsb, B_WIDTH // LANES),
        in_specs=[bias_spec, blk(SUPER, cur), blk(HALO, prv), blk(SUPER, cur), blk(HALO, nxt),
                  blk(HALO, prv), blk(SUPER, cur), blk(HALO, nxt)],
        out_specs=blk(SUPER, cur),
        out_shape=jax.ShapeDtypeStruct((b, t, B_WIDTH), F32),
        scratch_shapes=[stat, kv, kv, stat, stat, stat, stat, stat, stat],
        compiler_params=pltpu.CompilerParams(dimension_semantics=("parallel", "parallel", "parallel"),
                                             vmem_limit_bytes=VMEM_LIMIT),
        name="attention",
    )(bias, q, k, k, k, v, v, v)


def _out_kernel(x_ref, ya_ref, att_ref, gnb_ref, wout_ref, gffn_ref, wr_ref, br_ref, tri_ref,
                x2_ref, ei_ref, gc_ref, cnt_ref, base_ref):
    tm = x_ref.shape[0]

    @pl.when(pl.program_id(0) == 0)
    def _():
        base_ref[...] = jnp.zeros_like(base_ref)

    yb = _rms(att_ref[...], gnb_ref[...]).astype(BF16)
    a = jnp.concatenate([ya_ref[...], yb], axis=1)
    x2 = x_ref[...] + jnp.dot(a, wout_ref[...], preferred_element_type=F32)
    x2_ref[...] = x2
    xn = _rms(x2, gffn_ref[...])

    lg = lax.dot_general(wr_ref[...], xn.astype(BF16), NT_DIMS, preferred_element_type=F32) + br_ref[...]
    e_log = lg[0:N_EXPERTS]
    g_log = lg[N_EXPERTS:N_EXPERTS + N_GROUPS]
    r4 = lax.broadcasted_iota(jnp.int32, (N_GROUPS, tm), 0).astype(F32)
    g_max = jnp.max(g_log, axis=0, keepdims=True)
    g_sel = jnp.min(jnp.where(g_log == g_max, r4, float(N_GROUPS)), axis=0, keepdims=True)
    p_group = 1.0 / jnp.sum(jnp.exp(g_log - g_max), axis=0, keepdims=True)
    e_sel = jnp.zeros((EXPERTS_PER_GROUP, tm), F32)
    for g in range(N_GROUPS):
        e_sel = jnp.where(g_sel == float(g), e_log[g * EXPERTS_PER_GROUP:(g + 1) * EXPERTS_PER_GROUP], e_sel)
    r8 = lax.broadcasted_iota(jnp.int32, (EXPERTS_PER_GROUP, tm), 0).astype(F32)
    v1 = jnp.max(e_sel, axis=0, keepdims=True)
    i1 = jnp.min(jnp.where(e_sel == v1, r8, float(EXPERTS_PER_GROUP)), axis=0, keepdims=True)
    e_rest = jnp.where(r8 == i1, -jnp.inf, e_sel)
    v2 = jnp.max(e_rest, axis=0, keepdims=True)
    i2 = jnp.min(jnp.where(e_rest == v2, r8, float(EXPERTS_PER_GROUP)), axis=0, keepdims=True)
    d = jnp.exp(v2 - v1)
    gate1 = p_group * (1.0 / (1.0 + d))
    gate2 = p_group * (d / (1.0 + d))
    eid1 = g_sel * float(EXPERTS_PER_GROUP) + i1
    eid2 = g_sel * float(EXPERTS_PER_GROUP) + i2

    r32 = lax.broadcasted_iota(jnp.int32, (N_EXPERTS, tm), 0).astype(F32)
    oh1 = r32 == eid1
    oh2 = r32 == eid2
    oh1f = jnp.where(oh1, 1.0, 0.0)
    oh2f = jnp.where(oh2, 1.0, 0.0)
    tri = tri_ref[...]
    pre1 = jnp.dot(oh1f.astype(BF16), tri, preferred_element_type=F32)
    pre2 = jnp.dot(oh2f.astype(BF16), tri, preferred_element_type=F32)
    tot1 = jnp.sum(oh1f, axis=1, keepdims=True)
    tot2 = jnp.sum(oh2f, axis=1, keepdims=True)
    base_full = base_ref[...]
    base = base_full[:, 0:1]
    rank1 = jnp.sum(jnp.where(oh1, base + pre1, 0.0), axis=0, keepdims=True)
    rank2 = jnp.sum(jnp.where(oh2, base + tot1 + pre2, 0.0), axis=0, keepdims=True)
    base_full = base_full + tot1 + tot2
    base_ref[...] = base_full
    cnt_ref[...] = base_full
    ei_ref[...] = jnp.concatenate([eid1, eid2, rank1, rank2], axis=0).astype(jnp.int32)
    r128 = lax.broadcasted_iota(jnp.int32, (LANES, tm), 0)
    gates_rows = jnp.where(r128 == 0, gate1, jnp.where(r128 == 1, gate2, 0.0))
    gc_ref[...] = gates_rows.T


def _out_call(x2d, ya, att, gnb, wout, gffn, wr, br, tri):
    n = x2d.shape[0]
    tm = TM_OUT
    full = lambda shape: pl.BlockSpec(shape, lambda i: (0,) * len(shape))
    tok = lambda w: pl.BlockSpec((tm, w), lambda i: (i, 0))
    return pl.pallas_call(
        _out_kernel,
        grid=(n // tm,),
        in_specs=[tok(D_MODEL), tok(A_WIDTH), tok(B_WIDTH), full((1, B_WIDTH)), full((D_MODEL, D_MODEL)),
                  full((1, D_MODEL)), full((ROUTER_ROWS, D_MODEL)), full((ROUTER_ROWS, 1)), full((tm, tm))],
        out_specs=[tok(D_MODEL), pl.BlockSpec((4, tm), lambda i: (0, i)), tok(LANES),
                   full((N_EXPERTS, LANES))],
        out_shape=[jax.ShapeDtypeStruct((n, D_MODEL), F32),
                   jax.ShapeDtypeStruct((4, n), jnp.int32), jax.ShapeDtypeStruct((n, LANES), F32),
                   jax.ShapeDtypeStruct((N_EXPERTS, LANES), F32)],
        scratch_shapes=[pltpu.VMEM((N_EXPERTS, LANES), F32)],
        compiler_params=pltpu.CompilerParams(dimension_semantics=("arbitrary",), vmem_limit_bytes=VMEM_LIMIT),
        name="out_router",
    )(x2d, ya, att, gnb, wout, gffn, wr, br, tri)


def _row_copy(src, s, dst, d, sem):
    return pltpu.make_async_copy(src.at[pl.ds(s, 1), :], dst.at[pl.ds(d, 1), :], sem)


def _for_each_row(tm, fn):
    if MOVE_UNROLL >= tm:
        for t in range(tm):
            fn(t)
        return

    def body(g, c):
        for i in range(MOVE_UNROLL):
            fn(g * MOVE_UNROLL + i)
        return c

    lax.fori_loop(0, tm // MOVE_UNROLL, body, 0)


def _dispatch_kernel(zb_ref, zon_ref, d0_ref, d1_ref, x2_ref, gffn_ref, rows_ref, xn_buf, zero_ref, sems, zsem):
    tm = x2_ref.shape[0]
    step = pl.program_id(0)
    slot = step % 2
    xn_ref = xn_buf.at[slot]
    sem = sems.at[slot]
    xn_ref[...] = _rms(x2_ref[...], gffn_ref[...])

    @pl.when(pl.program_id(0) == 0)
    def _():
        zero_ref[...] = jnp.zeros_like(zero_ref)

        def zero_copy(i):
            start = pl.multiple_of(zb_ref[i] * ROW_BLOCK, ROW_BLOCK)
            return pltpu.make_async_copy(zero_ref, rows_ref.at[pl.ds(start, ROW_BLOCK), :], zsem)

        def start(i, c):
            @pl.when(zon_ref[i] == 1)
            def _():
                zero_copy(i).start()
            return c

        def wait(i, c):
            @pl.when(zon_ref[i] == 1)
            def _():
                zero_copy(i).wait()
            return c

        lax.fori_loop(0, 2 * N_EXPERTS, start, 0)
        lax.fori_loop(0, 2 * N_EXPERTS, wait, 0)

    def issue(t):
        _row_copy(xn_ref, t, rows_ref, d0_ref[t], sem).start()
        _row_copy(xn_ref, t, rows_ref, d1_ref[t], sem).start()

    _for_each_row(tm, issue)

    def wait_tile(s):
        tile = pltpu.make_async_copy(xn_buf.at[s], rows_ref.at[pl.ds(0, tm), :], sems.at[s])
        tile.wait()
        tile.wait()

    @pl.when(step > 0)
    def _():
        wait_tile(1 - slot)

    @pl.when(step == pl.num_programs(0) - 1)
    def _():
        wait_tile(slot)


def _dispatch_call(zero_blocks, zero_on, dest0, dest1, x2, gffn, n_rows):
    n = x2.shape[0]
    tm = TM_MOVE
    idx = pl.BlockSpec((tm,), lambda i, zb, zon: (i,), memory_space=pltpu.SMEM)
    grid_spec = pltpu.PrefetchScalarGridSpec(
        num_scalar_prefetch=2,
        grid=(n // tm,),
        in_specs=[idx, idx, pl.BlockSpec((tm, D_MODEL), lambda i, zb, zon: (i, 0)),
                  pl.BlockSpec((1, D_MODEL), lambda i, zb, zon: (0, 0))],
        out_specs=pl.BlockSpec(memory_space=pl.ANY),
        scratch_shapes=[pltpu.VMEM((2, tm, D_MODEL), F32), pltpu.VMEM((ROW_BLOCK, D_MODEL), F32),
                        pltpu.SemaphoreType.DMA((2,)), pltpu.SemaphoreType.DMA],
    )
    return pl.pallas_call(
        _dispatch_kernel,
        grid_spec=grid_spec,
        out_shape=jax.ShapeDtypeStruct((n_rows, D_MODEL), F32),
        compiler_params=pltpu.CompilerParams(dimension_semantics=("arbitrary",), has_side_effects=True,
                                             disable_bounds_checks=True),
        name="dispatch",
    )(zero_blocks, zero_on, dest0, dest1, x2, gffn)


def _expert_kernel(be_ref, bv_ref, bf_ref, x_ref, wg_ref, wu_ref, wd_ref, y_ref, wg_s, wu_s, wd_s):
    del be_ref
    b = pl.program_id(0)
    valid = bv_ref[b]

    @pl.when(bf_ref[b] == 1)
    def _():
        wg_s[...] = wg_ref[...].astype(BF16)
        wu_s[...] = wu_ref[...].astype(BF16)
        wd_s[...] = wd_ref[...].astype(BF16)

    @pl.when(valid > 0)
    def _():
        xb = x_ref[...].astype(BF16)
        g = jnp.dot(xb, wg_s[...], preferred_element_type=F32)
        u = jnp.dot(xb, wu_s[...], preferred_element_type=F32)
        h = (jax.nn.silu(g) * u).astype(BF16)
        y_ref[...] = jnp.dot(h, wd_s[...], preferred_element_type=F32)

    @pl.when(valid == 0)
    def _():
        y_ref[...] = jnp.zeros_like(y_ref)


def _expert_call(block_e, block_valid, block_first, x_rows, wg, wu, wd):
    n_rows = x_rows.shape[0]
    nb = n_rows // ROW_BLOCK
    weight = lambda shape: pl.BlockSpec((None,) + shape, lambda b, be, bv, bf: (be[b], 0, 0))
    grid_spec = pltpu.PrefetchScalarGridSpec(
        num_scalar_prefetch=3,
        grid=(nb,),
        in_specs=[pl.BlockSpec((ROW_BLOCK, D_MODEL), lambda b, be, bv, bf: (b, 0)),
                  weight((D_MODEL, D_EXPERT)), weight((D_MODEL, D_EXPERT)), weight((D_EXPERT, D_MODEL))],
        out_specs=pl.BlockSpec((ROW_BLOCK, D_MODEL), lambda b, be, bv, bf: (b, 0)),
        scratch_shapes=[pltpu.VMEM((D_MODEL, D_EXPERT), BF16), pltpu.VMEM((D_MODEL, D_EXPERT), BF16),
                        pltpu.VMEM((D_EXPERT, D_MODEL), BF16)],
    )
    return pl.pallas_call(
        _expert_kernel,
        grid_spec=grid_spec,
        out_shape=jax.ShapeDtypeStruct((n_rows, D_MODEL), F32),
        compiler_params=pltpu.CompilerParams(dimension_semantics=("arbitrary",), vmem_limit_bytes=VMEM_LIMIT),
        name="experts",
    )(block_e, block_valid, block_first, x_rows, wg, wu, wd)


def _combine_kernel(d0_ref, d1_ref, n0_ref, n1_ref, x2_ref, gc_ref, y_ref, o_ref, y0_buf, y1_buf, sems):
    tm = x2_ref.shape[0]
    step = pl.program_id(0)
    slot = step % 2

    def gather(i0_ref, i1_ref, s):
        def issue(t):
            _row_copy(y_ref, i0_ref[t], y0_buf.at[s], t, sems.at[s]).start()
            _row_copy(y_ref, i1_ref[t], y1_buf.at[s], t, sems.at[s]).start()

        _for_each_row(tm, issue)

    @pl.when(step == 0)
    def _():
        gather(d0_ref, d1_ref, slot)

    @pl.when(step < pl.num_programs(0) - 1)
    def _():
        gather(n0_ref, n1_ref, 1 - slot)

    pltpu.make_async_copy(y_ref.at[pl.ds(0, tm), :], y0_buf.at[slot], sems.at[slot]).wait()
    pltpu.make_async_copy(y_ref.at[pl.ds(0, tm), :], y1_buf.at[slot], sems.at[slot]).wait()
    gc = gc_ref[...]
    o_ref[...] = x2_ref[...] + (gc[:, 0:1] * y0_buf[slot] + gc[:, 1:2] * y1_buf[slot])


def _combine_call(dest0, dest1, x2, gc, y_rows):
    n = x2.shape[0]
    tm = TM_MOVE
    last = n // tm - 1
    idx = pl.BlockSpec((tm,), lambda i: (i,), memory_space=pltpu.SMEM)
    idx_next = pl.BlockSpec((tm,), lambda i: (jnp.minimum(i + 1, last),), memory_space=pltpu.SMEM)
    return pl.pallas_call(
        _combine_kernel,
        grid=(n // tm,),
        in_specs=[idx, idx, idx_next, idx_next, pl.BlockSpec((tm, D_MODEL), lambda i: (i, 0)),
                  pl.BlockSpec((tm, LANES), lambda i: (i, 0)), pl.BlockSpec(memory_space=pl.ANY)],
        out_specs=pl.BlockSpec((tm, D_MODEL), lambda i: (i, 0)),
        out_shape=jax.ShapeDtypeStruct((n, D_MODEL), F32),
        scratch_shapes=[pltpu.VMEM((2, tm, D_MODEL), F32), pltpu.VMEM((2, tm, D_MODEL), F32),
                        pltpu.SemaphoreType.DMA((2,))],
        compiler_params=pltpu.CompilerParams(dimension_semantics=("arbitrary",), vmem_limit_bytes=VMEM_LIMIT,
                                             disable_bounds_checks=True),
        name="combine",
    )(dest0, dest1, dest0, dest1, x2, gc, y_rows)


def _prepare(norm_mix, w_in, a_v_norm, a_spatial_w, a_spatial_b, q_norm, k_norm, out_norm_a, out_norm_b, w_out,
             norm_ffn, w_router_group, b_router_group, w_router_expert, b_router_expert,
             w_expert_gate, w_expert_up, w_expert_down):
    ch = jnp.arange(A_WIDTH) // GROUP_DIM
    pad = ROUTER_ROWS - N_EXPERTS - N_GROUPS
    dist = jnp.abs(jnp.arange(TK)[None, :] - jnp.arange(TQ)[:, None] - HALF).astype(F32)
    slope = 2.0 ** (-8.0 * (jnp.arange(HEADS, dtype=F32) + 1.0) / HEADS)
    dil = jnp.asarray(DILATIONS, F32)
    abias = jnp.where(dist <= HALF, -slope[None, :, None, None] * (dist * dil[:, None, None, None]), NEG_INF)
    abias = abias.reshape(len(DILATIONS), B_WIDTH // LANES, 2, TQ, TK).transpose(1, 0, 2, 3, 4)
    return dict(
        gmix=norm_mix.reshape(1, D_MODEL),
        win=w_in.astype(BF16),
        bd=(ch[:, None] == ch[None, :]).astype(BF16),
        avn=a_v_norm.reshape(1, A_WIDTH),
        wcat=jnp.concatenate([a_spatial_w[0::2], a_spatial_w[1::2]], axis=2).astype(BF16),
        bias=jnp.repeat(a_spatial_b.T, GROUP_DIM, axis=1),
        gq=(jnp.tile(q_norm, HEADS) * (HEAD_DIM ** -0.5)).reshape(1, B_WIDTH),
        gk=jnp.tile(k_norm, HEADS).reshape(1, B_WIDTH),
        gna=out_norm_a.reshape(1, A_WIDTH),
        gnb=out_norm_b.reshape(1, B_WIDTH),
        abias=abias,
        wout=w_out.astype(BF16),
        gffn=norm_ffn.reshape(1, D_MODEL),
        wr=jnp.concatenate([w_router_expert.T, w_router_group.T, jnp.zeros((pad, D_MODEL), F32)], axis=0).astype(BF16),
        br=jnp.concatenate([b_router_expert, b_router_group, jnp.zeros((pad,), F32)]).reshape(ROUTER_ROWS, 1),
        tri=(jnp.arange(TM_OUT)[:, None] < jnp.arange(TM_OUT)[None, :]).astype(BF16),
        wg=w_expert_gate,
        wu=w_expert_up,
        wd=w_expert_down,
    )


def _layer(x, p):
    b, t, _ = x.shape
    n = b * t
    x2d = x.reshape(n, D_MODEL)
    ya, q, k, v = _proj_call(x2d, p["gmix"], p["win"], p["bd"], p["avn"], p["wcat"], p["bias"], p["gq"], p["gk"], p["gna"])
    q, k, v = (a.reshape(b, t, B_WIDTH) for a in (q, k, v))
    att = _attn_call(q, k, v, p["abias"]).reshape(n, B_WIDTH)
    x2, ei, gc, cnt = _out_call(x2d, ya, att, p["gnb"], p["wout"], p["gffn"], p["wr"], p["br"], p["tri"])

    counts = cnt[:, 0].astype(jnp.int32)
    pcounts = (counts + ROW_BLOCK - 1) // ROW_BLOCK * ROW_BLOCK
    pends = jnp.cumsum(pcounts)
    pstarts = pends - pcounts
    nb = (2 * n) // ROW_BLOCK + N_EXPERTS
    starts = jnp.arange(nb, dtype=jnp.int32) * ROW_BLOCK
    block_e = jnp.minimum(jnp.sum((pends[None, :] <= starts[:, None]).astype(jnp.int32), axis=1), N_EXPERTS - 1)
    block_valid = jnp.clip(pstarts[block_e] + counts[block_e] - starts, 0, ROW_BLOCK)
    block_first = jnp.concatenate([jnp.ones((1,), jnp.int32), (block_e[1:] != block_e[:-1]).astype(jnp.int32)])
    dest0 = pstarts[ei[0]] + ei[2]
    dest1 = pstarts[ei[1]] + ei[3]
    used = pends[N_EXPERTS - 1] // ROW_BLOCK
    tail = used + jnp.arange(N_EXPERTS, dtype=jnp.int32)
    zero_blocks = jnp.concatenate([jnp.maximum(pends // ROW_BLOCK - 1, 0), jnp.minimum(tail, nb - 1)]).astype(jnp.int32)
    zero_on = jnp.concatenate([pcounts > 0, tail < nb]).astype(jnp.int32)

    x_rows = _dispatch_call(zero_blocks, zero_on, dest0, dest1, x2, p["gffn"], nb * ROW_BLOCK)
    y_rows = _expert_call(block_e, block_valid, block_first, x_rows, p["wg"], p["wu"], p["wd"])
    out = _combine_call(dest0, dest1, x2, gc, y_rows)
    return out.reshape(b, t, D_MODEL)


def kernel(x_prompt, x_sample, norm_mix, w_in, a_v_norm, a_spatial_w, a_spatial_b, q_norm, k_norm, out_norm_a,
           out_norm_b, w_out, norm_ffn, w_router_group, b_router_group, w_router_expert, b_router_expert,
           w_expert_gate, w_expert_up, w_expert_down):
    depth = norm_mix.shape[0]
    layers = [
        _prepare(norm_mix[l], w_in[l], a_v_norm[l], a_spatial_w[l], a_spatial_b[l], q_norm[l], k_norm[l],
                 out_norm_a[l], out_norm_b[l], w_out[l], norm_ffn[l], w_router_group[l], b_router_group[l],
                 w_router_expert[l], b_router_expert[l], w_expert_gate[l], w_expert_up[l], w_expert_down[l])
        for l in range(depth)
    ]

    def run(x):
        for p in layers:
            x = _layer(x, p)
        return x

    return (run(x_prompt), run(x_sample))
```

```python
import jax
import jax.numpy as jnp
from jax import lax
from jax.experimental import pallas as pl
from jax.experimental.pallas import tpu as pltpu

D_MODEL = 1024
A_WIDTH = 512
B_WIDTH = 512
IN_WIDTH = 2 * A_WIDTH + 3 * B_WIDTH
A_GROUPS = 8
GROUP_DIM = 64
CHUNK = 128
HEADS = 8
HEAD_DIM = 64
DILATIONS = (1, 4, 16)
HALF = 64
N_GROUPS = 4
EXPERTS_PER_GROUP = 8
N_EXPERTS = 32
D_EXPERT = 512
ROW_BLOCK = 512
EPS = 1e-6
NEG_INF = -1e30

LANES = 128
ROUTER_ROWS = 48
TM_PROJ = 512
TM_OUT = 512
TM_MOVE = 512
MOVE_UNROLL = 512
TQ = 128
TK = TQ + 2 * HALF
SUPER = TQ * max(DILATIONS)
HALO = HALF * max(DILATIONS)
VMEM_LIMIT = 48 * 1024 * 1024

F32 = jnp.float32
BF16 = jnp.bfloat16
NT_DIMS = (((1,), (1,)), ((), ()))


def _rms(x, gain):
    return x * lax.rsqrt(jnp.mean(x * x, axis=-1, keepdims=True) + EPS) * gain


def _proj_kernel(x_ref, gmix_ref, win_ref, bd_ref, avn_ref, wcat_ref, bias_ref, gq_ref, gk_ref, gna_ref,
                 ya_ref, q_ref, k_ref, v_ref):
    h = _rms(x_ref[...], gmix_ref[...])
    proj = jnp.dot(h.astype(BF16), win_ref[...], preferred_element_type=F32)
    pu = proj[:, 0:A_WIDTH]
    pv = proj[:, A_WIDTH:2 * A_WIDTH]
    q = proj[:, 2 * A_WIDTH:2 * A_WIDTH + B_WIDTH]
    k = proj[:, 2 * A_WIDTH + B_WIDTH:2 * A_WIDTH + 2 * B_WIDTH]
    v = proj[:, 2 * A_WIDTH + 2 * B_WIDTH:]
    bd = bd_ref[...]

    def group_norm(t, gain):
        ms = jnp.dot((t * t).astype(BF16), bd, preferred_element_type=F32) * (1.0 / GROUP_DIM)
        return t * lax.rsqrt(ms + EPS) * gain

    u = jax.nn.gelu(pu)
    vn = group_norm(jax.nn.gelu(pv), avn_ref[...]).astype(BF16)
    lane = lax.broadcasted_iota(jnp.int32, (CHUNK, LANES), 1)
    lo = lane < GROUP_DIM
    zero = jnp.zeros((CHUNK, LANES), BF16)
    chunks = []
    for c in range(x_ref.shape[0] // CHUNK):
        blks = []
        for j in range(A_WIDTH // LANES):
            vb = vn[c * CHUNK:(c + 1) * CHUNK, j * LANES:(j + 1) * LANES]
            rhs = jnp.concatenate([jnp.where(lo, vb, zero), jnp.where(lo, zero, vb)], axis=0)
            blks.append(jnp.dot(wcat_ref[j], rhs, preferred_element_type=F32))
        chunks.append(jnp.concatenate(blks, axis=1) + bias_ref[...])
    mixed = jnp.concatenate(chunks, axis=0)
    ya_ref[...] = _rms(u * mixed, gna_ref[...]).astype(BF16)
    q_ref[...] = group_norm(q, gq_ref[...]).astype(BF16)
    k_ref[...] = group_norm(k, gk_ref[...]).astype(BF16)
    v_ref[...] = v.astype(BF16)


def _proj_call(x2d, gmix, win, bd, avn, wcat, bias, gq, gk, gna):
    n = x2d.shape[0]
    tm = TM_PROJ
    full = lambda shape: pl.BlockSpec(shape, lambda i: (0,) * len(shape))
    tok = lambda w: pl.BlockSpec((tm, w), lambda i: (i, 0))
    return pl.pallas_call(
        _proj_kernel,
        grid=(n // tm,),
        in_specs=[tok(D_MODEL), full((1, D_MODEL)), full((D_MODEL, IN_WIDTH)), full((A_WIDTH, A_WIDTH)),
                  full((1, A_WIDTH)), full((A_WIDTH // LANES, CHUNK, 2 * CHUNK)), full((CHUNK, A_WIDTH)),
                  full((1, B_WIDTH)), full((1, B_WIDTH)), full((1, A_WIDTH))],
        out_specs=[tok(A_WIDTH), tok(B_WIDTH), tok(B_WIDTH), tok(B_WIDTH)],
        out_shape=[jax.ShapeDtypeStruct((n, A_WIDTH), BF16)] + [jax.ShapeDtypeStruct((n, B_WIDTH), BF16)] * 3,
        compiler_params=pltpu.CompilerParams(dimension_semantics=("parallel",), vmem_limit_bytes=VMEM_LIMIT),
        name="proj_gating",
    )(x2d, gmix, win, bd, avn, wcat, bias, gq, gk, gna)


def _attn_kernel(bias_ref, q_ref, kp_ref, kc_ref, kn_ref, vp_ref, vc_ref, vn_ref, o_ref,
                 qf, kf, vf, acc16, m16, l16, acc4, m4, l4):
    sb = pl.program_id(1)

    qf[...] = q_ref[...].astype(F32)
    kf[0:HALO, :] = kp_ref[...].astype(F32)
    kf[HALO:HALO + SUPER, :] = kc_ref[...].astype(F32)
    kf[HALO + SUPER:, :] = kn_ref[...].astype(F32)
    vf[0:HALO, :] = vp_ref[...].astype(F32)
    vf[HALO:HALO + SUPER, :] = vc_ref[...].astype(F32)
    vf[HALO + SUPER:, :] = vn_ref[...].astype(F32)

    lane = lax.broadcasted_iota(jnp.int32, (TQ, LANES), 1)
    lo = lane < HEAD_DIM

    def branch_unit(dil, res, qs):
        n_sub = SUPER // dil
        q_start = dil * qs + res
        k_start = HALO + dil * (qs - HALF) + res
        if dil == 1:
            qb = qf[pl.ds(q_start, TQ), :]
            kb = kf[pl.ds(k_start, TK), :]
            vb = vf[pl.ds(k_start, TK), :]
        else:
            qb = qf[pl.ds(q_start, TQ, stride=dil), :]
            kb = kf[pl.ds(k_start, TK, stride=dil), :]
            vb = vf[pl.ds(k_start, TK, stride=dil), :]
        qb = qb.astype(BF16)
        kb = kb.astype(BF16)
        vb = vb.astype(BF16)
        branch = DILATIONS.index(dil)
        edge = qs == 0 or qs + TQ == n_sub
        if edge:
            col = lax.broadcasted_iota(jnp.int32, (1, TK), 1)
            c_lo = jnp.where(sb == 0, HALF - qs, 0)
            c_hi = jnp.where(sb == pl.num_programs(1) - 1, n_sub - qs + HALF, TK)
            in_seq = (col >= c_lo) & (col < c_hi)
        res_h = []
        for hh in range(2):
            qm = jnp.where(lo if hh == 0 else ~lo, qb, jnp.zeros_like(qb))
            s = lax.dot_general(qm, kb, NT_DIMS, preferred_element_type=F32)
            s = s + bias_ref[branch, hh]
            if edge:
                s = jnp.where(in_seq, s, NEG_INF)
            m = jnp.max(s, axis=-1, keepdims=True)
            p = jnp.exp(s - m)
            l = jnp.sum(p, axis=-1, keepdims=True)
            pv = jnp.dot(p.astype(BF16), vb, preferred_element_type=F32)
            res_h.append((pv, m, l))
        return tuple(jnp.where(lo, a, b) for a, b in zip(res_h[0], res_h[1]))

    def strided_branch(dil, acc_ref, m_ref, l_ref):
        for u in range(SUPER // TQ):
            res = u % dil
            qs = (u // dil) * TQ
            acc, m, l = branch_unit(dil, res, qs)
            rows = pl.ds(dil * qs + res, TQ, stride=dil)
            acc_ref[rows, :] = acc
            m_ref[rows, :] = m
            l_ref[rows, :] = l

    strided_branch(16, acc16, m16, l16)
    strided_branch(4, acc4, m4, l4)

    for u in range(SUPER // TQ):
        qs = u * TQ
        a1, m1, l1 = branch_unit(1, 0, qs)
        rows = pl.ds(qs, TQ)
        m_4, m_16 = m4[rows, :], m16[rows, :]
        m_all = jnp.maximum(jnp.maximum(m1, m_4), m_16)
        e1 = jnp.exp(m1 - m_all)
        e4 = jnp.exp(m_4 - m_all)
        e16 = jnp.exp(m_16 - m_all)
        num = e1 * a1 + e4 * acc4[rows, :] + e16 * acc16[rows, :]
        den = e1 * l1 + e4 * l4[rows, :] + e16 * l16[rows, :]
        o_ref[rows, :] = num / den


def _attn_call(q, k, v, bias):
    b, t, _ = q.shape
    nsb = t // SUPER
    nkb = t // HALO
    per = SUPER // HALO
    cur = lambda bb, s, j: (bb, s, j)
    prv = lambda bb, s, j: (bb, jnp.maximum(s * per - 1, 0), j)
    nxt = lambda bb, s, j: (bb, jnp.minimum((s + 1) * per, nkb - 1), j)
    blk = lambda rows, f: pl.BlockSpec((None, rows, LANES), f)
    bias_spec = pl.BlockSpec((None,) + bias.shape[1:], lambda bb, s, j: (j, 0, 0, 0, 0))
    stat = pltpu.VMEM((SUPER, LANES), F32)
    kv = pltpu.VMEM((SUPER + 2 * HALO, LANES), F32)
    return pl.pallas_call(
        _attn_kernel,
        grid=(b, nsb, B_WIDTH // LANES),
        in_specs=[bias_spec, blk(SUPER, cur), blk(HALO, prv), blk(SUPER, cur), blk(HALO, nxt),
                  blk(HALO, prv), blk(SUPER, cur), blk(HALO, nxt)],
        out_specs=blk(SUPER, cur),
        out_shape=jax.ShapeDtypeStruct((b, t, B_WIDTH), F32),
        scratch_shapes=[stat, kv, kv, stat, stat, stat, stat, stat, stat],
        compiler_params=pltpu.CompilerParams(dimension_semantics=("parallel", "parallel", "parallel"),
                                             vmem_limit_bytes=VMEM_LIMIT),
        name="attention",
    )(bias, q, k, k, k, v, v, v)


def _out_kernel(x_ref, ya_ref, att_ref, gnb_ref, wout_ref, gffn_ref, wr_ref, br_ref, tri_ref,
                x2_ref, ei_ref, gc_ref, cnt_ref, base_ref):
    tm = x_ref.shape[0]

    @pl.when(pl.program_id(0) == 0)
    def _():
        base_ref[...] = jnp.zeros_like(base_ref)

    yb = _rms(att_ref[...], gnb_ref[...]).astype(BF16)
    a = jnp.concatenate([ya_ref[...], yb], axis=1)
    x2 = x_ref[...] + jnp.dot(a, wout_ref[...], preferred_element_type=F32)
    x2_ref[...] = x2
    xn = _rms(x2, gffn_ref[...])

    lg = lax.dot_general(wr_ref[...], xn.astype(BF16), NT_DIMS, preferred_element_type=F32) + br_ref[...]
    e_log = lg[0:N_EXPERTS]
    g_log = lg[N_EXPERTS:N_EXPERTS + N_GROUPS]
    r4 = lax.broadcasted_iota(jnp.int32, (N_GROUPS, tm), 0).astype(F32)
    g_max = jnp.max(g_log, axis=0, keepdims=True)
    g_sel = jnp.min(jnp.where(g_log == g_max, r4, float(N_GROUPS)), axis=0, keepdims=True)
    p_group = 1.0 / jnp.sum(jnp.exp(g_log - g_max), axis=0, keepdims=True)
    e_sel = jnp.zeros((EXPERTS_PER_GROUP, tm), F32)
    for g in range(N_GROUPS):
        e_sel = jnp.where(g_sel == float(g), e_log[g * EXPERTS_PER_GROUP:(g + 1) * EXPERTS_PER_GROUP], e_sel)
    r8 = lax.broadcasted_iota(jnp.int32, (EXPERTS_PER_GROUP, tm), 0).astype(F32)
    v1 = jnp.max(e_sel, axis=0, keepdims=True)
    i1 = jnp.min(jnp.where(e_sel == v1, r8, float(EXPERTS_PER_GROUP)), axis=0, keepdims=True)
    e_rest = jnp.where(r8 == i1, -jnp.inf, e_sel)
    v2 = jnp.max(e_rest, axis=0, keepdims=True)
    i2 = jnp.min(jnp.where(e_rest == v2, r8, float(EXPERTS_PER_GROUP)), axis=0, keepdims=True)
    d = jnp.exp(v2 - v1)
    gate1 = p_group * (1.0 / (1.0 + d))
    gate2 = p_group * (d / (1.0 + d))
    eid1 = g_sel * float(EXPERTS_PER_GROUP) + i1
    eid2 = g_sel * float(EXPERTS_PER_GROUP) + i2

    r32 = lax.broadcasted_iota(jnp.int32, (N_EXPERTS, tm), 0).astype(F32)
    oh1 = r32 == eid1
    oh2 = r32 == eid2
    oh1f = jnp.where(oh1, 1.0, 0.0)
    oh2f = jnp.where(oh2, 1.0, 0.0)
    tri = tri_ref[...]
    pre1 = jnp.dot(oh1f.astype(BF16), tri, preferred_element_type=F32)
    pre2 = jnp.dot(oh2f.astype(BF16), tri, preferred_element_type=F32)
    tot1 = jnp.sum(oh1f, axis=1, keepdims=True)
    tot2 = jnp.sum(oh2f, axis=1, keepdims=True)
    base_full = base_ref[...]
    base = base_full[:, 0:1]
    rank1 = jnp.sum(jnp.where(oh1, base + pre1, 0.0), axis=0, keepdims=True)
    rank2 = jnp.sum(jnp.where(oh2, base + tot1 + pre2, 0.0), axis=0, keepdims=True)
    base_full = base_full + tot1 + tot2
    base_ref[...] = base_full
    cnt_ref[...] = base_full
    ei_ref[...] = jnp.concatenate([eid1, eid2, rank1, rank2], axis=0).astype(jnp.int32)
    r128 = lax.broadcasted_iota(jnp.int32, (LANES, tm), 0)
    gates_rows = jnp.where(r128 == 0, gate1, jnp.where(r128 == 1, gate2, 0.0))
    gc_ref[...] = gates_rows.T


def _out_call(x2d, ya, att, gnb, wout, gffn, wr, br, tri):
    n = x2d.shape[0]
    tm = TM_OUT
    full = lambda shape: pl.BlockSpec(shape, lambda i: (0,) * len(shape))
    tok = lambda w: pl.BlockSpec((tm, w), lambda i: (i, 0))
    return pl.pallas_call(
        _out_kernel,
        grid=(n // tm,),
        in_specs=[tok(D_MODEL), tok(A_WIDTH), tok(B_WIDTH), full((1, B_WIDTH)), full((D_MODEL, D_MODEL)),
                  full((1, D_MODEL)), full((ROUTER_ROWS, D_MODEL)), full((ROUTER_ROWS, 1)), full((tm, tm))],
        out_specs=[tok(D_MODEL), pl.BlockSpec((4, tm), lambda i: (0, i)), tok(LANES),
                   full((N_EXPERTS, LANES))],
        out_shape=[jax.ShapeDtypeStruct((n, D_MODEL), F32),
                   jax.ShapeDtypeStruct((4, n), jnp.int32), jax.ShapeDtypeStruct((n, LANES), F32),
                   jax.ShapeDtypeStruct((N_EXPERTS, LANES), F32)],
        scratch_shapes=[pltpu.VMEM((N_EXPERTS, LANES), F32)],
        compiler_params=pltpu.CompilerParams(dimension_semantics=("arbitrary",), vmem_limit_bytes=VMEM_LIMIT),
        name="out_router",
    )(x2d, ya, att, gnb, wout, gffn, wr, br, tri)


def _row_copy(src, s, dst, d, sem):
    return pltpu.make_async_copy(src.at[pl.ds(s, 1), :], dst.at[pl.ds(d, 1), :], sem)


def _for_each_row(tm, fn):
    if MOVE_UNROLL >= tm:
        for t in range(tm):
            fn(t)
        return

    def body(g, c):
        for i in range(MOVE_UNROLL):
            fn(g * MOVE_UNROLL + i)
        return c

    lax.fori_loop(0, tm // MOVE_UNROLL, body, 0)


def _dispatch_kernel(zb_ref, zon_ref, d0_ref, d1_ref, x2_ref, gffn_ref, rows_ref, xn_buf, zero_ref, sems, zsem):
    tm = x2_ref.shape[0]
    step = pl.program_id(0)
    slot = step % 2
    xn_ref = xn_buf.at[slot]
    sem = sems.at[slot]
    xn_ref[...] = _rms(x2_ref[...], gffn_ref[...])

    @pl.when(pl.program_id(0) == 0)
    def _():
        zero_ref[...] = jnp.zeros_like(zero_ref)

        def zero_copy(i):
            start = pl.multiple_of(zb_ref[i] * ROW_BLOCK, ROW_BLOCK)
            return pltpu.make_async_copy(zero_ref, rows_ref.at[pl.ds(start, ROW_BLOCK), :], zsem)

        def start(i, c):
            @pl.when(zon_ref[i] == 1)
            def _():
                zero_copy(i).start()
            return c

        def wait(i, c):
            @pl.when(zon_ref[i] == 1)
            def _():
                zero_copy(i).wait()
            return c

        lax.fori_loop(0, 2 * N_EXPERTS, start, 0)
        lax.fori_loop(0, 2 * N_EXPERTS, wait, 0)

    def issue(t):
        _row_copy(xn_ref, t, rows_ref, d0_ref[t], sem).start()
        _row_copy(xn_ref, t, rows_ref, d1_ref[t], sem).start()

    _for_each_row(tm, issue)

    def wait_tile(s):
        tile = pltpu.make_async_copy(xn_buf.at[s], rows_ref.at[pl.ds(0, tm), :], sems.at[s])
        tile.wait()
        tile.wait()

    @pl.when(step > 0)
    def _():
        wait_tile(1 - slot)

    @pl.when(step == pl.num_programs(0) - 1)
    def _():
        wait_tile(slot)


def _dispatch_call(zero_blocks, zero_on, dest0, dest1, x2, gffn, n_rows):
    n = x2.shape[0]
    tm = TM_MOVE
    idx = pl.BlockSpec((tm,), lambda i, zb, zon: (i,), memory_space=pltpu.SMEM)
    grid_spec = pltpu.PrefetchScalarGridSpec(
        num_scalar_prefetch=2,
        grid=(n // tm,),
        in_specs=[idx, idx, pl.BlockSpec((tm, D_MODEL), lambda i, zb, zon: (i, 0)),
                  pl.BlockSpec((1, D_MODEL), lambda i, zb, zon: (0, 0))],
        out_specs=pl.BlockSpec(memory_space=pl.ANY),
        scratch_shapes=[pltpu.VMEM((2, tm, D_MODEL), F32), pltpu.VMEM((ROW_BLOCK, D_MODEL), F32),
                        pltpu.SemaphoreType.DMA((2,)), pltpu.SemaphoreType.DMA],
    )
    return pl.pallas_call(
        _dispatch_kernel,
        grid_spec=grid_spec,
        out_shape=jax.ShapeDtypeStruct((n_rows, D_MODEL), F32),
        compiler_params=pltpu.CompilerParams(dimension_semantics=("arbitrary",), has_side_effects=True,
                                             disable_bounds_checks=True),
        name="dispatch",
    )(zero_blocks, zero_on, dest0, dest1, x2, gffn)


def _expert_kernel(be_ref, bv_ref, bf_ref, x_ref, wg_ref, wu_ref, wd_ref, y_ref, wg_s, wu_s, wd_s):
    del be_ref
    b = pl.program_id(0)
    valid = bv_ref[b]

    @pl.when(bf_ref[b] == 1)
    def _():
        wg_s[...] = wg_ref[...].astype(BF16)
        wu_s[...] = wu_ref[...].astype(BF16)
        wd_s[...] = wd_ref[...].astype(BF16)

    @pl.when(valid > 0)
    def _():
        xb = x_ref[...].astype(BF16)
        g = jnp.dot(xb, wg_s[...], preferred_element_type=F32)
        u = jnp.dot(xb, wu_s[...], preferred_element_type=F32)
        h = (jax.nn.silu(g) * u).astype(BF16)
        y_ref[...] = jnp.dot(h, wd_s[...], preferred_element_type=F32)

    @pl.when(valid == 0)
    def _():
        y_ref[...] = jnp.zeros_like(y_ref)


def _expert_call(block_e, block_valid, block_first, x_rows, wg, wu, wd):
    n_rows = x_rows.shape[0]
    nb = n_rows // ROW_BLOCK
    weight = lambda shape: pl.BlockSpec((None,) + shape, lambda b, be, bv, bf: (be[b], 0, 0))
    grid_spec = pltpu.PrefetchScalarGridSpec(
        num_scalar_prefetch=3,
        grid=(nb,),
        in_specs=[pl.BlockSpec((ROW_BLOCK, D_MODEL), lambda b, be, bv, bf: (b, 0)),
                  weight((D_MODEL, D_EXPERT)), weight((D_MODEL, D_EXPERT)), weight((D_EXPERT, D_MODEL))],
        out_specs=pl.BlockSpec((ROW_BLOCK, D_MODEL), lambda b, be, bv, bf: (b, 0)),
        scratch_shapes=[pltpu.VMEM((D_MODEL, D_EXPERT), BF16), pltpu.VMEM((D_MODEL, D_EXPERT), BF16),
                        pltpu.VMEM((D_EXPERT, D_MODEL), BF16)],
    )
    return pl.pallas_call(
        _expert_kernel,
        grid_spec=grid_spec,
        out_shape=jax.ShapeDtypeStruct((n_rows, D_MODEL), F32),
        compiler_params=pltpu.CompilerParams(dimension_semantics=("arbitrary",), vmem_limit_bytes=VMEM_LIMIT),
        name="experts",
    )(block_e, block_valid, block_first, x_rows, wg, wu, wd)


def _combine_kernel(d0_ref, d1_ref, n0_ref, n1_ref, x2_ref, gc_ref, y_ref, o_ref, y0_buf, y1_buf, sems):
    tm = x2_ref.shape[0]
    step = pl.program_id(0)
    slot = step % 2

    def gather(i0_ref, i1_ref, s):
        def issue(t):
            _row_copy(y_ref, i0_ref[t], y0_buf.at[s], t, sems.at[s]).start()
            _row_copy(y_ref, i1_ref[t], y1_buf.at[s], t, sems.at[s]).start()

        _for_each_row(tm, issue)

    @pl.when(step == 0)
    def _():
        gather(d0_ref, d1_ref, slot)

    @pl.when(step < pl.num_programs(0) - 1)
    def _():
        gather(n0_ref, n1_ref, 1 - slot)

    pltpu.make_async_copy(y_ref.at[pl.ds(0, tm), :], y0_buf.at[slot], sems.at[slot]).wait()
    pltpu.make_async_copy(y_ref.at[pl.ds(0, tm), :], y1_buf.at[slot], sems.at[slot]).wait()
    gc = gc_ref[...]
    o_ref[...] = x2_ref[...] + (gc[:, 0:1] * y0_buf[slot] + gc[:, 1:2] * y1_buf[slot])


def _combine_call(dest0, dest1, x2, gc, y_rows):
    n = x2.shape[0]
    tm = TM_MOVE
    last = n // tm - 1
    idx = pl.BlockSpec((tm,), lambda i: (i,), memory_space=pltpu.SMEM)
    idx_next = pl.BlockSpec((tm,), lambda i: (jnp.minimum(i + 1, last),), memory_space=pltpu.SMEM)
    return pl.pallas_call(
        _combine_kernel,
        grid=(n // tm,),
        in_specs=[idx, idx, idx_next, idx_next, pl.BlockSpec((tm, D_MODEL), lambda i: (i, 0)),
                  pl.BlockSpec((tm, LANES), lambda i: (i, 0)), pl.BlockSpec(memory_space=pl.ANY)],
        out_specs=pl.BlockSpec((tm, D_MODEL), lambda i: (i, 0)),
        out_shape=jax.ShapeDtypeStruct((n, D_MODEL), F32),
        scratch_shapes=[pltpu.VMEM((2, tm, D_MODEL), F32), pltpu.VMEM((2, tm, D_MODEL), F32),
                        pltpu.SemaphoreType.DMA((2,))],
        compiler_params=pltpu.CompilerParams(dimension_semantics=("arbitrary",), vmem_limit_bytes=VMEM_LIMIT,
                                             disable_bounds_checks=True),
        name="combine",
    )(dest0, dest1, dest0, dest1, x2, gc, y_rows)


def _prepare(norm_mix, w_in, a_v_norm, a_spatial_w, a_spatial_b, q_norm, k_norm, out_norm_a, out_norm_b, w_out,
             norm_ffn, w_router_group, b_router_group, w_router_expert, b_router_expert,
             w_expert_gate, w_expert_up, w_expert_down):
    ch = jnp.arange(A_WIDTH) // GROUP_DIM
    pad = ROUTER_ROWS - N_EXPERTS - N_GROUPS
    dist = jnp.abs(jnp.arange(TK)[None, :] - jnp.arange(TQ)[:, None] - HALF).astype(F32)
    slope = 2.0 ** (-8.0 * (jnp.arange(HEADS, dtype=F32) + 1.0) / HEADS)
    dil = jnp.asarray(DILATIONS, F32)
    abias = jnp.where(dist <= HALF, -slope[None, :, None, None] * (dist * dil[:, None, None, None]), NEG_INF)
    abias = abias.reshape(len(DILATIONS), B_WIDTH // LANES, 2, TQ, TK).transpose(1, 0, 2, 3, 4)
    return dict(
        gmix=norm_mix.reshape(1, D_MODEL),
        win=w_in.astype(BF16),
        bd=(ch[:, None] == ch[None, :]).astype(BF16),
        avn=a_v_norm.reshape(1, A_WIDTH),
        wcat=jnp.concatenate([a_spatial_w[0::2], a_spatial_w[1::2]], axis=2).astype(BF16),
        bias=jnp.repeat(a_spatial_b.T, GROUP_DIM, axis=1),
        gq=(jnp.tile(q_norm, HEADS) * (HEAD_DIM ** -0.5)).reshape(1, B_WIDTH),
        gk=jnp.tile(k_norm, HEADS).reshape(1, B_WIDTH),
        gna=out_norm_a.reshape(1, A_WIDTH),
        gnb=out_norm_b.reshape(1, B_WIDTH),
        abias=abias,
        wout=w_out.astype(BF16),
        gffn=norm_ffn.reshape(1, D_MODEL),
        wr=jnp.concatenate([w_router_expert.T, w_router_group.T, jnp.zeros((pad, D_MODEL), F32)], axis=0).astype(BF16),
        br=jnp.concatenate([b_router_expert, b_router_group, jnp.zeros((pad,), F32)]).reshape(ROUTER_ROWS, 1),
        tri=(jnp.arange(TM_OUT)[:, None] < jnp.arange(TM_OUT)[None, :]).astype(BF16),
        wg=w_expert_gate,
        wu=w_expert_up,
        wd=w_expert_down,
    )


def _layer(x, p):
    b, t, _ = x.shape
    n = b * t
    x2d = x.reshape(n, D_MODEL)
    ya, q, k, v = _proj_call(x2d, p["gmix"], p["win"], p["bd"], p["avn"], p["wcat"], p["bias"], p["gq"], p["gk"], p["gna"])
    q, k, v = (a.reshape(b, t, B_WIDTH) for a in (q, k, v))
    att = _attn_call(q, k, v, p["abias"]).reshape(n, B_WIDTH)
    x2, ei, gc, cnt = _out_call(x2d, ya, att, p["gnb"], p["wout"], p["gffn"], p["wr"], p["br"], p["tri"])

    counts = cnt[:, 0].astype(jnp.int32)
    pcounts = (counts + ROW_BLOCK - 1) // ROW_BLOCK * ROW_BLOCK
    pends = jnp.cumsum(pcounts)
    pstarts = pends - pcounts
    nb = (2 * n) // ROW_BLOCK + N_EXPERTS
    starts = jnp.arange(nb, dtype=jnp.int32) * ROW_BLOCK
    block_e = jnp.minimum(jnp.sum((pends[None, :] <= starts[:, None]).astype(jnp.int32), axis=1), N_EXPERTS - 1)
    overlap = (jnp.minimum(starts[:, None] + ROW_BLOCK, (pstarts + counts)[None, :])
               - jnp.maximum(starts[:, None], pstarts[None, :]))
    block_valid = jnp.sum(jnp.clip(overlap, 0, ROW_BLOCK), axis=1)
    block_first = jnp.concatenate([jnp.ones((1,), jnp.int32), (block_e[1:] != block_e[:-1]).astype(jnp.int32)])
    expert_ids = jnp.arange(N_EXPERTS, dtype=jnp.int32)[:, None]
    row_start = lambda eid: jnp.sum(jnp.where(eid[None, :] == expert_ids, pstarts[:, None], 0), axis=0)
    dest0 = row_start(ei[0]) + ei[2]
    dest1 = row_start(ei[1]) + ei[3]
    used = pends[N_EXPERTS - 1] // ROW_BLOCK
    tail = used + jnp.arange(N_EXPERTS, dtype=jnp.int32)
    zero_blocks = jnp.concatenate([jnp.maximum(pends // ROW_BLOCK - 1, 0), jnp.minimum(tail, nb - 1)]).astype(jnp.int32)
    zero_on = jnp.concatenate([pcounts > 0, tail < nb]).astype(jnp.int32)

    x_rows = _dispatch_call(zero_blocks, zero_on, dest0, dest1, x2, p["gffn"], nb * ROW_BLOCK)
    y_rows = _expert_call(block_e, block_valid, block_first, x_rows, p["wg"], p["wu"], p["wd"])
    out = _combine_call(dest0, dest1, x2, gc, y_rows)
    return out.reshape(b, t, D_MODEL)


def kernel(x_prompt, x_sample, norm_mix, w_in, a_v_norm, a_spatial_w, a_spatial_b, q_norm, k_norm, out_norm_a,
           out_norm_b, w_out, norm_ffn, w_router_group, b_router_group, w_router_expert, b_router_expert,
           w_expert_gate, w_expert_up, w_expert_down):
    depth = norm_mix.shape[0]
    layers = [
        _prepare(norm_mix[l], w_in[l], a_v_norm[l], a_spatial_w[l], a_spatial_b[l], q_norm[l], k_norm[l],
                 out_norm_a[l], out_norm_b[l], w_out[l], norm_ffn[l], w_router_group[l], b_router_group[l],
                 w_router_expert[l], b_router_expert[l], w_expert_gate[l], w_expert_up[l], w_expert_down[l])
        for l in range(depth)
    ]

    def run(x):
        for p in layers:
            x = _layer(x, p)
        return x

    return (run(x_prompt), run(x_sample))
```

```python
import jax
import jax.numpy as jnp
from jax import lax
from jax.experimental import pallas as pl
from jax.experimental.pallas import tpu as pltpu

D_MODEL = 1024
A_WIDTH = 512
B_WIDTH = 512
IN_WIDTH = 2 * A_WIDTH + 3 * B_WIDTH
A_GROUPS = 8
GROUP_DIM = 64
CHUNK = 128
HEADS = 8
HEAD_DIM = 64
DILATIONS = (1, 4, 16)
HALF = 64
N_GROUPS = 4
EXPERTS_PER_GROUP = 8
N_EXPERTS = 32
D_EXPERT = 512
ROW_BLOCK = 512
EPS = 1e-6
NEG_INF = -1e30

LANES = 128
ROUTER_ROWS = 48
TM_PROJ = 512
TM_OUT = 512
TM_MOVE = 512
MOVE_UNROLL = 512
TQ = 128
TK = TQ + 2 * HALF
SUPER = TQ * max(DILATIONS)
HALO = HALF * max(DILATIONS)
VMEM_LIMIT = 48 * 1024 * 1024

F32 = jnp.float32
BF16 = jnp.bfloat16
NT_DIMS = (((1,), (1,)), ((), ()))


def _rms(x, gain):
    return x * lax.rsqrt(jnp.mean(x * x, axis=-1, keepdims=True) + EPS) * gain


def _proj_kernel(x_ref, gmix_ref, win_ref, bd_ref, avn_ref, wcat_ref, bias_ref, gq_ref, gk_ref, gna_ref,
                 ya_ref, q_ref, k_ref, v_ref):
    h = _rms(x_ref[...], gmix_ref[...])
    proj = jnp.dot(h.astype(BF16), win_ref[...], preferred_element_type=F32)
    pu = proj[:, 0:A_WIDTH]
    pv = proj[:, A_WIDTH:2 * A_WIDTH]
    q = proj[:, 2 * A_WIDTH:2 * A_WIDTH + B_WIDTH]
    k = proj[:, 2 * A_WIDTH + B_WIDTH:2 * A_WIDTH + 2 * B_WIDTH]
    v = proj[:, 2 * A_WIDTH + 2 * B_WIDTH:]
    bd = bd_ref[...]

    def group_norm(t, gain):
        ms = jnp.dot((t * t).astype(BF16), bd, preferred_element_type=F32) * (1.0 / GROUP_DIM)
        return t * lax.rsqrt(ms + EPS) * gain

    u = jax.nn.gelu(pu)
    vn = group_norm(jax.nn.gelu(pv), avn_ref[...]).astype(BF16)
    lane = lax.broadcasted_iota(jnp.int32, (CHUNK, LANES), 1)
    lo = lane < GROUP_DIM
    zero = jnp.zeros((CHUNK, LANES), BF16)
    chunks = []
    for c in range(x_ref.shape[0] // CHUNK):
        blks = []
        for j in range(A_WIDTH // LANES):
            vb = vn[c * CHUNK:(c + 1) * CHUNK, j * LANES:(j + 1) * LANES]
            rhs = jnp.concatenate([jnp.where(lo, vb, zero), jnp.where(lo, zero, vb)], axis=0)
            blks.append(jnp.dot(wcat_ref[j], rhs, preferred_element_type=F32))
        chunks.append(jnp.concatenate(blks, axis=1) + bias_ref[...])
    mixed = jnp.concatenate(chunks, axis=0)
    ya_ref[...] = _rms(u * mixed, gna_ref[...]).astype(BF16)
    q_ref[...] = group_norm(q, gq_ref[...]).astype(BF16)
    k_ref[...] = group_norm(k, gk_ref[...]).astype(BF16)
    v_ref[...] = v.astype(BF16)


def _proj_call(x2d, gmix, win, bd, avn, wcat, bias, gq, gk, gna):
    n = x2d.shape[0]
    tm = TM_PROJ
    full = lambda shape: pl.BlockSpec(shape, lambda i: (0,) * len(shape))
    tok = lambda w: pl.BlockSpec((tm, w), lambda i: (i, 0))
    return pl.pallas_call(
        _proj_kernel,
        grid=(n // tm,),
        in_specs=[tok(D_MODEL), full((1, D_MODEL)), full((D_MODEL, IN_WIDTH)), full((A_WIDTH, A_WIDTH)),
                  full((1, A_WIDTH)), full((A_WIDTH // LANES, CHUNK, 2 * CHUNK)), full((CHUNK, A_WIDTH)),
                  full((1, B_WIDTH)), full((1, B_WIDTH)), full((1, A_WIDTH))],
        out_specs=[tok(A_WIDTH), tok(B_WIDTH), tok(B_WIDTH), tok(B_WIDTH)],
        out_shape=[jax.ShapeDtypeStruct((n, A_WIDTH), BF16)] + [jax.ShapeDtypeStruct((n, B_WIDTH), BF16)] * 3,
        compiler_params=pltpu.CompilerParams(dimension_semantics=("parallel",), vmem_limit_bytes=VMEM_LIMIT),
        name="proj_gating",
    )(x2d, gmix, win, bd, avn, wcat, bias, gq, gk, gna)


def _attn_kernel(bias_ref, q_ref, kp_ref, kc_ref, kn_ref, vp_ref, vc_ref, vn_ref, o_ref,
                 qf, kf, vf, acc16, m16, l16, acc4, m4, l4):
    sb = pl.program_id(1)

    qf[...] = q_ref[...].astype(F32)
    kf[0:HALO, :] = kp_ref[...].astype(F32)
    kf[HALO:HALO + SUPER, :] = kc_ref[...].astype(F32)
    kf[HALO + SUPER:, :] = kn_ref[...].astype(F32)
    vf[0:HALO, :] = vp_ref[...].astype(F32)
    vf[HALO:HALO + SUPER, :] = vc_ref[...].astype(F32)
    vf[HALO + SUPER:, :] = vn_ref[...].astype(F32)

    lane = lax.broadcasted_iota(jnp.int32, (TQ, LANES), 1)
    lo = lane < HEAD_DIM

    def branch_unit(dil, res, qs):
        n_sub = SUPER // dil
        q_start = dil * qs + res
        k_start = HALO + dil * (qs - HALF) + res
        if dil == 1:
            qb = qf[pl.ds(q_start, TQ), :]
            kb = kf[pl.ds(k_start, TK), :]
            vb = vf[pl.ds(k_start, TK), :]
        else:
            qb = qf[pl.ds(q_start, TQ, stride=dil), :]
            kb = kf[pl.ds(k_start, TK, stride=dil), :]
            vb = vf[pl.ds(k_start, TK, stride=dil), :]
        qb = qb.astype(BF16)
        kb = kb.astype(BF16)
        vb = vb.astype(BF16)
        branch = DILATIONS.index(dil)
        edge = qs == 0 or qs + TQ == n_sub
        if edge:
            col = lax.broadcasted_iota(jnp.int32, (1, TK), 1)
            c_lo = jnp.where(sb == 0, HALF - qs, 0)
            c_hi = jnp.where(sb == pl.num_programs(1) - 1, n_sub - qs + HALF, TK)
            in_seq = (col >= c_lo) & (col < c_hi)
        res_h = []
        for hh in range(2):
            qm = jnp.where(lo if hh == 0 else ~lo, qb, jnp.zeros_like(qb))
            s = lax.dot_general(qm, kb, NT_DIMS, preferred_element_type=F32)
            s = s + bias_ref[branch, hh]
            if edge:
                s = jnp.where(in_seq, s, NEG_INF)
            m = jnp.max(s, axis=-1, keepdims=True)
            p = jnp.exp(s - m)
            l = jnp.sum(p, axis=-1, keepdims=True)
            pv = jnp.dot(p.astype(BF16), vb, preferred_element_type=F32)
            res_h.append((pv, m, l))
        return tuple(jnp.where(lo, a, b) for a, b in zip(res_h[0], res_h[1]))

    def strided_branch(dil, acc_ref, m_ref, l_ref):
        for u in range(SUPER // TQ):
            res = u % dil
            qs = (u // dil) * TQ
            acc, m, l = branch_unit(dil, res, qs)
            rows = pl.ds(dil * qs + res, TQ, stride=dil)
            acc_ref[rows, :] = acc
            m_ref[rows, :] = m
            l_ref[rows, :] = l

    strided_branch(16, acc16, m16, l16)
    strided_branch(4, acc4, m4, l4)

    for u in range(SUPER // TQ):
        qs = u * TQ
        a1, m1, l1 = branch_unit(1, 0, qs)
        rows = pl.ds(qs, TQ)
        m_4, m_16 = m4[rows, :], m16[rows, :]
        m_all = jnp.maximum(jnp.maximum(m1, m_4), m_16)
        e1 = jnp.exp(m1 - m_all)
        e4 = jnp.exp(m_4 - m_all)
        e16 = jnp.exp(m_16 - m_all)
        num = e1 * a1 + e4 * acc4[rows, :] + e16 * acc16[rows, :]
        den = e1 * l1 + e4 * l4[rows, :] + e16 * l16[rows, :]
        o_ref[rows, :] = num / den


def _attn_call(q, k, v, bias):
    b, t, _ = q.shape
    nsb = t // SUPER
    nkb = t // HALO
    per = SUPER // HALO
    cur = lambda bb, s, j: (bb, s, j)
    prv = lambda bb, s, j: (bb, jnp.maximum(s * per - 1, 0), j)
    nxt = lambda bb, s, j: (bb, jnp.minimum((s + 1) * per, nkb - 1), j)
    blk = lambda rows, f: pl.BlockSpec((None, rows, LANES), f)
    bias_spec = pl.BlockSpec((None,) + bias.shape[1:], lambda bb, s, j: (j, 0, 0, 0, 0))
    stat = pltpu.VMEM((SUPER, LANES), F32)
    kv = pltpu.VMEM((SUPER + 2 * HALO, LANES), F32)
    return pl.pallas_call(
        _attn_kernel,
        grid=(b, nsb, B_WIDTH // LANES),
        in_specs=[bias_spec, blk(SUPER, cur), blk(HALO, prv), blk(SUPER, cur), blk(HALO, nxt),
                  blk(HALO, prv), blk(SUPER, cur), blk(HALO, nxt)],
        out_specs=blk(SUPER, cur),
        out_shape=jax.ShapeDtypeStruct((b, t, B_WIDTH), F32),
        scratch_shapes=[stat, kv, kv, stat, stat, stat, stat, stat, stat],
        compiler_params=pltpu.CompilerParams(dimension_semantics=("parallel", "parallel", "parallel"),
                                             vmem_limit_bytes=VMEM_LIMIT),
        name="attention",
    )(bias, q, k, k, k, v, v, v)


def _out_kernel(x_ref, ya_ref, att_ref, gnb_ref, wout_ref, gffn_ref, wr_ref, br_ref, tri_ref,
                x2_ref, ei_ref, gc_ref, cnt_ref, base_ref):
    tm = x_ref.shape[0]

    @pl.when(pl.program_id(0) == 0)
    def _():
        base_ref[...] = jnp.zeros_like(base_ref)

    yb = _rms(att_ref[...], gnb_ref[...]).astype(BF16)
    a = jnp.concatenate([ya_ref[...], yb], axis=1)
    x2 = x_ref[...] + jnp.dot(a, wout_ref[...], preferred_element_type=F32)
    x2_ref[...] = x2
    xn = _rms(x2, gffn_ref[...])

    lg = lax.dot_general(wr_ref[...], xn.astype(BF16), NT_DIMS, preferred_element_type=F32) + br_ref[...]
    e_log = lg[0:N_EXPERTS]
    g_log = lg[N_EXPERTS:N_EXPERTS + N_GROUPS]
    r4 = lax.broadcasted_iota(jnp.int32, (N_GROUPS, tm), 0).astype(F32)
    g_max = jnp.max(g_log, axis=0, keepdims=True)
    g_sel = jnp.min(jnp.where(g_log == g_max, r4, float(N_GROUPS)), axis=0, keepdims=True)
    p_group = 1.0 / jnp.sum(jnp.exp(g_log - g_max), axis=0, keepdims=True)
    e_sel = jnp.zeros((EXPERTS_PER_GROUP, tm), F32)
    for g in range(N_GROUPS):
        e_sel = jnp.where(g_sel == float(g), e_log[g * EXPERTS_PER_GROUP:(g + 1) * EXPERTS_PER_GROUP], e_sel)
    r8 = lax.broadcasted_iota(jnp.int32, (EXPERTS_PER_GROUP, tm), 0).astype(F32)
    v1 = jnp.max(e_sel, axis=0, keepdims=True)
    i1 = jnp.min(jnp.where(e_sel == v1, r8, float(EXPERTS_PER_GROUP)), axis=0, keepdims=True)
    e_rest = jnp.where(r8 == i1, -jnp.inf, e_sel)
    v2 = jnp.max(e_rest, axis=0, keepdims=True)
    i2 = jnp.min(jnp.where(e_rest == v2, r8, float(EXPERTS_PER_GROUP)), axis=0, keepdims=True)
    d = jnp.exp(v2 - v1)
    gate1 = p_group * (1.0 / (1.0 + d))
    gate2 = p_group * (d / (1.0 + d))
    eid1 = g_sel * float(EXPERTS_PER_GROUP) + i1
    eid2 = g_sel * float(EXPERTS_PER_GROUP) + i2

    r32 = lax.broadcasted_iota(jnp.int32, (N_EXPERTS, tm), 0).astype(F32)
    oh1 = r32 == eid1
    oh2 = r32 == eid2
    oh1f = jnp.where(oh1, 1.0, 0.0)
    oh2f = jnp.where(oh2, 1.0, 0.0)
    tri = tri_ref[...]
    pre1 = jnp.dot(oh1f.astype(BF16), tri, preferred_element_type=F32)
    pre2 = jnp.dot(oh2f.astype(BF16), tri, preferred_element_type=F32)
    tot1 = jnp.sum(oh1f, axis=1, keepdims=True)
    tot2 = jnp.sum(oh2f, axis=1, keepdims=True)
    base_full = base_ref[...]
    base = base_full[:, 0:1]
    rank1 = jnp.sum(jnp.where(oh1, base + pre1, 0.0), axis=0, keepdims=True)
    rank2 = jnp.sum(jnp.where(oh2, base + tot1 + pre2, 0.0), axis=0, keepdims=True)
    base_full = base_full + tot1 + tot2
    base_ref[...] = base_full
    cnt_ref[...] = base_full
    ei_ref[...] = jnp.concatenate([eid1, eid2, rank1, rank2], axis=0).astype(jnp.int32)
    r128 = lax.broadcasted_iota(jnp.int32, (LANES, tm), 0)
    gates_rows = jnp.where(r128 == 0, gate1, jnp.where(r128 == 1, gate2, 0.0))
    gc_ref[...] = gates_rows.T


def _out_call(x2d, ya, att, gnb, wout, gffn, wr, br, tri):
    n = x2d.shape[0]
    tm = TM_OUT
    full = lambda shape: pl.BlockSpec(shape, lambda i: (0,) * len(shape))
    tok = lambda w: pl.BlockSpec((tm, w), lambda i: (i, 0))
    return pl.pallas_call(
        _out_kernel,
        grid=(n // tm,),
        in_specs=[tok(D_MODEL), tok(A_WIDTH), tok(B_WIDTH), full((1, B_WIDTH)), full((D_MODEL, D_MODEL)),
                  full((1, D_MODEL)), full((ROUTER_ROWS, D_MODEL)), full((ROUTER_ROWS, 1)), full((tm, tm))],
        out_specs=[tok(D_MODEL), pl.BlockSpec((4, tm), lambda i: (0, i)), tok(LANES),
                   full((N_EXPERTS, LANES))],
        out_shape=[jax.ShapeDtypeStruct((n, D_MODEL), F32),
                   jax.ShapeDtypeStruct((4, n), jnp.int32), jax.ShapeDtypeStruct((n, LANES), F32),
                   jax.ShapeDtypeStruct((N_EXPERTS, LANES), F32)],
        scratch_shapes=[pltpu.VMEM((N_EXPERTS, LANES), F32)],
        compiler_params=pltpu.CompilerParams(dimension_semantics=("arbitrary",), vmem_limit_bytes=VMEM_LIMIT),
        name="out_router",
    )(x2d, ya, att, gnb, wout, gffn, wr, br, tri)


def _row_copy(src, s, dst, d, sem):
    return pltpu.make_async_copy(src.at[pl.ds(s, 1), :], dst.at[pl.ds(d, 1), :], sem)


def _for_each_row(tm, fn):
    if MOVE_UNROLL >= tm:
        for t in range(tm):
            fn(t)
        return

    def body(g, c):
        for i in range(MOVE_UNROLL):
            fn(g * MOVE_UNROLL + i)
        return c

    lax.fori_loop(0, tm // MOVE_UNROLL, body, 0)


def _dispatch_kernel(zb_ref, zon_ref, d0_ref, d1_ref, x2_ref, gffn_ref, rows_ref, xn_buf, zero_ref, sems, zsem):
    tm = x2_ref.shape[0]
    step = pl.program_id(0)
    slot = step % 2
    xn_ref = xn_buf.at[slot]
    sem = sems.at[slot]
    xn_ref[...] = _rms(x2_ref[...], gffn_ref[...])

    @pl.when(pl.program_id(0) == 0)
    def _():
        zero_ref[...] = jnp.zeros_like(zero_ref)

        def zero_copy(i):
            start = pl.multiple_of(zb_ref[i] * ROW_BLOCK, ROW_BLOCK)
            return pltpu.make_async_copy(zero_ref, rows_ref.at[pl.ds(start, ROW_BLOCK), :], zsem)

        def start(i, c):
            @pl.when(zon_ref[i] == 1)
            def _():
                zero_copy(i).start()
            return c

        def wait(i, c):
            @pl.when(zon_ref[i] == 1)
            def _():
                zero_copy(i).wait()
            return c

        lax.fori_loop(0, 2 * N_EXPERTS, start, 0)
        lax.fori_loop(0, 2 * N_EXPERTS, wait, 0)

    def issue(t):
        _row_copy(xn_ref, t, rows_ref, d0_ref[t], sem).start(priority=0)
        _row_copy(xn_ref, t, rows_ref, d1_ref[t], sem).start(priority=1)

    _for_each_row(tm, issue)

    def wait_tile(s):
        tile = pltpu.make_async_copy(xn_buf.at[s], rows_ref.at[pl.ds(0, tm), :], sems.at[s])
        tile.wait()
        tile.wait()

    @pl.when(step > 0)
    def _():
        wait_tile(1 - slot)

    @pl.when(step == pl.num_programs(0) - 1)
    def _():
        wait_tile(slot)


def _dispatch_call(zero_blocks, zero_on, dest0, dest1, x2, gffn, n_rows):
    n = x2.shape[0]
    tm = TM_MOVE
    idx = pl.BlockSpec((tm,), lambda i, zb, zon: (i,), memory_space=pltpu.SMEM)
    grid_spec = pltpu.PrefetchScalarGridSpec(
        num_scalar_prefetch=2,
        grid=(n // tm,),
        in_specs=[idx, idx, pl.BlockSpec((tm, D_MODEL), lambda i, zb, zon: (i, 0)),
                  pl.BlockSpec((1, D_MODEL), lambda i, zb, zon: (0, 0))],
        out_specs=pl.BlockSpec(memory_space=pl.ANY),
        scratch_shapes=[pltpu.VMEM((2, tm, D_MODEL), F32), pltpu.VMEM((ROW_BLOCK, D_MODEL), F32),
                        pltpu.SemaphoreType.DMA((2,)), pltpu.SemaphoreType.DMA],
    )
    return pl.pallas_call(
        _dispatch_kernel,
        grid_spec=grid_spec,
        out_shape=jax.ShapeDtypeStruct((n_rows, D_MODEL), F32),
        compiler_params=pltpu.CompilerParams(dimension_semantics=("arbitrary",), has_side_effects=True,
                                             disable_bounds_checks=True),
        name="dispatch",
    )(zero_blocks, zero_on, dest0, dest1, x2, gffn)


def _expert_kernel(be_ref, bv_ref, bf_ref, x_ref, wg_ref, wu_ref, wd_ref, y_ref, wg_s, wu_s, wd_s):
    del be_ref
    b = pl.program_id(0)
    valid = bv_ref[b]

    @pl.when(bf_ref[b] == 1)
    def _():
        wg_s[...] = wg_ref[...].astype(BF16)
        wu_s[...] = wu_ref[...].astype(BF16)
        wd_s[...] = wd_ref[...].astype(BF16)

    @pl.when(valid > 0)
    def _():
        xb = x_ref[...].astype(BF16)
        g = jnp.dot(xb, wg_s[...], preferred_element_type=F32)
        u = jnp.dot(xb, wu_s[...], preferred_element_type=F32)
        h = (jax.nn.silu(g) * u).astype(BF16)
        y_ref[...] = jnp.dot(h, wd_s[...], preferred_element_type=F32)

    @pl.when(valid == 0)
    def _():
        y_ref[...] = jnp.zeros_like(y_ref)


def _expert_call(block_e, block_valid, block_first, x_rows, wg, wu, wd):
    n_rows = x_rows.shape[0]
    nb = n_rows // ROW_BLOCK
    weight = lambda shape: pl.BlockSpec((None,) + shape, lambda b, be, bv, bf: (be[b], 0, 0))
    grid_spec = pltpu.PrefetchScalarGridSpec(
        num_scalar_prefetch=3,
        grid=(nb,),
        in_specs=[pl.BlockSpec((ROW_BLOCK, D_MODEL), lambda b, be, bv, bf: (b, 0)),
                  weight((D_MODEL, D_EXPERT)), weight((D_MODEL, D_EXPERT)), weight((D_EXPERT, D_MODEL))],
        out_specs=pl.BlockSpec((ROW_BLOCK, D_MODEL), lambda b, be, bv, bf: (b, 0)),
        scratch_shapes=[pltpu.VMEM((D_MODEL, D_EXPERT), BF16), pltpu.VMEM((D_MODEL, D_EXPERT), BF16),
                        pltpu.VMEM((D_EXPERT, D_MODEL), BF16)],
    )
    return pl.pallas_call(
        _expert_kernel,
        grid_spec=grid_spec,
        out_shape=jax.ShapeDtypeStruct((n_rows, D_MODEL), F32),
        compiler_params=pltpu.CompilerParams(dimension_semantics=("arbitrary",), vmem_limit_bytes=VMEM_LIMIT),
        name="experts",
    )(block_e, block_valid, block_first, x_rows, wg, wu, wd)


def _combine_kernel(d0_ref, d1_ref, n0_ref, n1_ref, x2_ref, gc_ref, y_ref, o_ref, y0_buf, y1_buf, sems):
    tm = x2_ref.shape[0]
    step = pl.program_id(0)
    slot = step % 2

    def gather(i0_ref, i1_ref, s):
        def issue(t):
            _row_copy(y_ref, i0_ref[t], y0_buf.at[s], t, sems.at[s]).start(priority=0)
            _row_copy(y_ref, i1_ref[t], y1_buf.at[s], t, sems.at[s]).start(priority=1)

        _for_each_row(tm, issue)

    @pl.when(step == 0)
    def _():
        gather(d0_ref, d1_ref, slot)

    @pl.when(step < pl.num_programs(0) - 1)
    def _():
        gather(n0_ref, n1_ref, 1 - slot)

    pltpu.make_async_copy(y_ref.at[pl.ds(0, tm), :], y0_buf.at[slot], sems.at[slot]).wait()
    pltpu.make_async_copy(y_ref.at[pl.ds(0, tm), :], y1_buf.at[slot], sems.at[slot]).wait()
    gc = gc_ref[...]
    o_ref[...] = x2_ref[...] + (gc[:, 0:1] * y0_buf[slot] + gc[:, 1:2] * y1_buf[slot])


def _combine_call(dest0, dest1, x2, gc, y_rows):
    n = x2.shape[0]
    tm = TM_MOVE
    last = n // tm - 1
    idx = pl.BlockSpec((tm,), lambda i: (i,), memory_space=pltpu.SMEM)
    idx_next = pl.BlockSpec((tm,), lambda i: (jnp.minimum(i + 1, last),), memory_space=pltpu.SMEM)
    return pl.pallas_call(
        _combine_kernel,
        grid=(n // tm,),
        in_specs=[idx, idx, idx_next, idx_next, pl.BlockSpec((tm, D_MODEL), lambda i: (i, 0)),
                  pl.BlockSpec((tm, LANES), lambda i: (i, 0)), pl.BlockSpec(memory_space=pl.ANY)],
        out_specs=pl.BlockSpec((tm, D_MODEL), lambda i: (i, 0)),
        out_shape=jax.ShapeDtypeStruct((n, D_MODEL), F32),
        scratch_shapes=[pltpu.VMEM((2, tm, D_MODEL), F32), pltpu.VMEM((2, tm, D_MODEL), F32),
                        pltpu.SemaphoreType.DMA((2,))],
        compiler_params=pltpu.CompilerParams(dimension_semantics=("arbitrary",), vmem_limit_bytes=VMEM_LIMIT,
                                             disable_bounds_checks=True),
        name="combine",
    )(dest0, dest1, dest0, dest1, x2, gc, y_rows)


def _prepare(norm_mix, w_in, a_v_norm, a_spatial_w, a_spatial_b, q_norm, k_norm, out_norm_a, out_norm_b, w_out,
             norm_ffn, w_router_group, b_router_group, w_router_expert, b_router_expert,
             w_expert_gate, w_expert_up, w_expert_down):
    ch = jnp.arange(A_WIDTH) // GROUP_DIM
    pad = ROUTER_ROWS - N_EXPERTS - N_GROUPS
    dist = jnp.abs(jnp.arange(TK)[None, :] - jnp.arange(TQ)[:, None] - HALF).astype(F32)
    slope = 2.0 ** (-8.0 * (jnp.arange(HEADS, dtype=F32) + 1.0) / HEADS)
    dil = jnp.asarray(DILATIONS, F32)
    abias = jnp.where(dist <= HALF, -slope[None, :, None, None] * (dist * dil[:, None, None, None]), NEG_INF)
    abias = abias.reshape(len(DILATIONS), B_WIDTH // LANES, 2, TQ, TK).transpose(1, 0, 2, 3, 4)
    return dict(
        gmix=norm_mix.reshape(1, D_MODEL),
        win=w_in.astype(BF16),
        bd=(ch[:, None] == ch[None, :]).astype(BF16),
        avn=a_v_norm.reshape(1, A_WIDTH),
        wcat=jnp.concatenate([a_spatial_w[0::2], a_spatial_w[1::2]], axis=2).astype(BF16),
        bias=jnp.repeat(a_spatial_b.T, GROUP_DIM, axis=1),
        gq=(jnp.tile(q_norm, HEADS) * (HEAD_DIM ** -0.5)).reshape(1, B_WIDTH),
        gk=jnp.tile(k_norm, HEADS).reshape(1, B_WIDTH),
        gna=out_norm_a.reshape(1, A_WIDTH),
        gnb=out_norm_b.reshape(1, B_WIDTH),
        abias=abias,
        wout=w_out.astype(BF16),
        gffn=norm_ffn.reshape(1, D_MODEL),
        wr=jnp.concatenate([w_router_expert.T, w_router_group.T, jnp.zeros((pad, D_MODEL), F32)], axis=0).astype(BF16),
        br=jnp.concatenate([b_router_expert, b_router_group, jnp.zeros((pad,), F32)]).reshape(ROUTER_ROWS, 1),
        tri=(jnp.arange(TM_OUT)[:, None] < jnp.arange(TM_OUT)[None, :]).astype(BF16),
        wg=w_expert_gate,
        wu=w_expert_up,
        wd=w_expert_down,
    )


def _layer(x, p):
    b, t, _ = x.shape
    n = b * t
    x2d = x.reshape(n, D_MODEL)
    ya, q, k, v = _proj_call(x2d, p["gmix"], p["win"], p["bd"], p["avn"], p["wcat"], p["bias"], p["gq"], p["gk"], p["gna"])
    q, k, v = (a.reshape(b, t, B_WIDTH) for a in (q, k, v))
    att = _attn_call(q, k, v, p["abias"]).reshape(n, B_WIDTH)
    x2, ei, gc, cnt = _out_call(x2d, ya, att, p["gnb"], p["wout"], p["gffn"], p["wr"], p["br"], p["tri"])

    counts = cnt[:, 0].astype(jnp.int32)
    pcounts = (counts + ROW_BLOCK - 1) // ROW_BLOCK * ROW_BLOCK
    pends = jnp.cumsum(pcounts)
    pstarts = pends - pcounts
    nb = (2 * n) // ROW_BLOCK + N_EXPERTS
    starts = jnp.arange(nb, dtype=jnp.int32) * ROW_BLOCK
    block_e = jnp.minimum(jnp.sum((pends[None, :] <= starts[:, None]).astype(jnp.int32), axis=1), N_EXPERTS - 1)
    overlap = (jnp.minimum(starts[:, None] + ROW_BLOCK, (pstarts + counts)[None, :])
               - jnp.maximum(starts[:, None], pstarts[None, :]))
    block_valid = jnp.sum(jnp.clip(overlap, 0, ROW_BLOCK), axis=1)
    block_first = jnp.concatenate([jnp.ones((1,), jnp.int32), (block_e[1:] != block_e[:-1]).astype(jnp.int32)])
    expert_ids = jnp.arange(N_EXPERTS, dtype=jnp.int32)[:, None]
    row_start = lambda eid: jnp.sum(jnp.where(eid[None, :] == expert_ids, pstarts[:, None], 0), axis=0)
    dest0 = row_start(ei[0]) + ei[2]
    dest1 = row_start(ei[1]) + ei[3]
    used = pends[N_EXPERTS - 1] // ROW_BLOCK
    tail = used + jnp.arange(N_EXPERTS, dtype=jnp.int32)
    zero_blocks = jnp.concatenate([jnp.maximum(pends // ROW_BLOCK - 1, 0), jnp.minimum(tail, nb - 1)]).astype(jnp.int32)
    zero_on = jnp.concatenate([pcounts > 0, tail < nb]).astype(jnp.int32)

    x_rows = _dispatch_call(zero_blocks, zero_on, dest0, dest1, x2, p["gffn"], nb * ROW_BLOCK)
    y_rows = _expert_call(block_e, block_valid, block_first, x_rows, p["wg"], p["wu"], p["wd"])
    out = _combine_call(dest0, dest1, x2, gc, y_rows)
    return out.reshape(b, t, D_MODEL)


def kernel(x_prompt, x_sample, norm_mix, w_in, a_v_norm, a_spatial_w, a_spatial_b, q_norm, k_norm, out_norm_a,
           out_norm_b, w_out, norm_ffn, w_router_group, b_router_group, w_router_expert, b_router_expert,
           w_expert_gate, w_expert_up, w_expert_down):
    depth = norm_mix.shape[0]
    layers = [
        _prepare(norm_mix[l], w_in[l], a_v_norm[l], a_spatial_w[l], a_spatial_b[l], q_norm[l], k_norm[l],
                 out_norm_a[l], out_norm_b[l], w_out[l], norm_ffn[l], w_router_group[l], b_router_group[l],
                 w_router_expert[l], b_router_expert[l], w_expert_gate[l], w_expert_up[l], w_expert_down[l])
        for l in range(depth)
    ]

    def run(x):
        for p in layers:
            x = _layer(x, p)
        return x

    return (run(x_prompt), run(x_sample))
```

```python
import jax
import jax.numpy as jnp
from jax import lax
from jax.experimental import pallas as pl
from jax.experimental.pallas import tpu as pltpu

D_MODEL = 1024
A_WIDTH = 512
B_WIDTH = 512
IN_WIDTH = 2 * A_WIDTH + 3 * B_WIDTH
A_GROUPS = 8
GROUP_DIM = 64
CHUNK = 128
HEADS = 8
HEAD_DIM = 64
DILATIONS = (1, 4, 16)
HALF = 64
N_GROUPS = 4
EXPERTS_PER_GROUP = 8
N_EXPERTS = 32
D_EXPERT = 512
ROW_BLOCK = 512
EPS = 1e-6
NEG_INF = -1e30

LANES = 128
ROUTER_ROWS = 48
TM_PROJ = 512
TM_OUT = 1024
SUB_OUT = 512
TM_MOVE = 512
MOVE_UNROLL = 512
TQ = 128
TK = TQ + 2 * HALF
SUPER = TQ * max(DILATIONS)
HALO = HALF * max(DILATIONS)
VMEM_LIMIT = 48 * 1024 * 1024

F32 = jnp.float32
BF16 = jnp.bfloat16
NT_DIMS = (((1,), (1,)), ((), ()))


def _rms(x, gain):
    return x * lax.rsqrt(jnp.mean(x * x, axis=-1, keepdims=True) + EPS) * gain


def _proj_kernel(x_ref, gmix_ref, win_ref, bd_ref, avn_ref, wcat_ref, bias_ref, gq_ref, gk_ref, gna_ref,
                 ya_ref, q_ref, k_ref, v_ref):
    h = _rms(x_ref[...], gmix_ref[...])
    proj = jnp.dot(h.astype(BF16), win_ref[...], preferred_element_type=F32)
    pu = proj[:, 0:A_WIDTH]
    pv = proj[:, A_WIDTH:2 * A_WIDTH]
    q = proj[:, 2 * A_WIDTH:2 * A_WIDTH + B_WIDTH]
    k = proj[:, 2 * A_WIDTH + B_WIDTH:2 * A_WIDTH + 2 * B_WIDTH]
    v = proj[:, 2 * A_WIDTH + 2 * B_WIDTH:]
    bd = bd_ref[...]

    def group_norm(t, gain):
        ms = jnp.dot((t * t).astype(BF16), bd, preferred_element_type=F32) * (1.0 / GROUP_DIM)
        return t * lax.rsqrt(ms + EPS) * gain

    u = jax.nn.gelu(pu)
    vn = group_norm(jax.nn.gelu(pv), avn_ref[...]).astype(BF16)
    lane = lax.broadcasted_iota(jnp.int32, (CHUNK, LANES), 1)
    lo = lane < GROUP_DIM
    zero = jnp.zeros((CHUNK, LANES), BF16)
    chunks = []
    for c in range(x_ref.shape[0] // CHUNK):
        blks = []
        for j in range(A_WIDTH // LANES):
            vb = vn[c * CHUNK:(c + 1) * CHUNK, j * LANES:(j + 1) * LANES]
            rhs = jnp.concatenate([jnp.where(lo, vb, zero), jnp.where(lo, zero, vb)], axis=0)
            blks.append(jnp.dot(wcat_ref[j], rhs, preferred_element_type=F32))
        chunks.append(jnp.concatenate(blks, axis=1) + bias_ref[...])
    mixed = jnp.concatenate(chunks, axis=0)
    ya_ref[...] = _rms(u * mixed, gna_ref[...]).astype(BF16)
    q_ref[...] = group_norm(q, gq_ref[...]).astype(BF16)
    k_ref[...] = group_norm(k, gk_ref[...]).astype(BF16)
    v_ref[...] = v.astype(BF16)


def _proj_call(x2d, gmix, win, bd, avn, wcat, bias, gq, gk, gna):
    n = x2d.shape[0]
    tm = TM_PROJ
    full = lambda shape: pl.BlockSpec(shape, lambda i: (0,) * len(shape))
    tok = lambda w: pl.BlockSpec((tm, w), lambda i: (i, 0))
    return pl.pallas_call(
        _proj_kernel,
        grid=(n // tm,),
        in_specs=[tok(D_MODEL), full((1, D_MODEL)), full((D_MODEL, IN_WIDTH)), full((A_WIDTH, A_WIDTH)),
                  full((1, A_WIDTH)), full((A_WIDTH // LANES, CHUNK, 2 * CHUNK)), full((CHUNK, A_WIDTH)),
                  full((1, B_WIDTH)), full((1, B_WIDTH)), full((1, A_WIDTH))],
        out_specs=[tok(A_WIDTH), tok(B_WIDTH), tok(B_WIDTH), tok(B_WIDTH)],
        out_shape=[jax.ShapeDtypeStruct((n, A_WIDTH), BF16)] + [jax.ShapeDtypeStruct((n, B_WIDTH), BF16)] * 3,
        compiler_params=pltpu.CompilerParams(dimension_semantics=("parallel",), vmem_limit_bytes=VMEM_LIMIT),
        name="proj_gating",
    )(x2d, gmix, win, bd, avn, wcat, bias, gq, gk, gna)


def _attn_kernel(bias_ref, q_ref, kp_ref, kc_ref, kn_ref, vp_ref, vc_ref, vn_ref, o_ref,
                 qf, kf, vf, acc16, m16, l16, acc4, m4, l4):
    sb = pl.program_id(1)

    qf[...] = q_ref[...].astype(F32)
    kf[0:HALO, :] = kp_ref[...].astype(F32)
    kf[HALO:HALO + SUPER, :] = kc_ref[...].astype(F32)
    kf[HALO + SUPER:, :] = kn_ref[...].astype(F32)
    vf[0:HALO, :] = vp_ref[...].astype(F32)
    vf[HALO:HALO + SUPER, :] = vc_ref[...].astype(F32)
    vf[HALO + SUPER:, :] = vn_ref[...].astype(F32)

    lane = lax.broadcasted_iota(jnp.int32, (TQ, LANES), 1)
    lo = lane < HEAD_DIM

    def branch_unit(dil, res, qs):
        n_sub = SUPER // dil
        q_start = dil * qs + res
        k_start = HALO + dil * (qs - HALF) + res
        if dil == 1:
            qb = qf[pl.ds(q_start, TQ), :]
            kb = kf[pl.ds(k_start, TK), :]
            vb = vf[pl.ds(k_start, TK), :]
        else:
            qb = qf[pl.ds(q_start, TQ, stride=dil), :]
            kb = kf[pl.ds(k_start, TK, stride=dil), :]
            vb = vf[pl.ds(k_start, TK, stride=dil), :]
        qb = qb.astype(BF16)
        kb = kb.astype(BF16)
        vb = vb.astype(BF16)
        branch = DILATIONS.index(dil)
        edge = qs == 0 or qs + TQ == n_sub
        if edge:
            col = lax.broadcasted_iota(jnp.int32, (1, TK), 1)
            c_lo = jnp.where(sb == 0, HALF - qs, 0)
            c_hi = jnp.where(sb == pl.num_programs(1) - 1, n_sub - qs + HALF, TK)
            in_seq = (col >= c_lo) & (col < c_hi)
        res_h = []
        for hh in range(2):
            qm = jnp.where(lo if hh == 0 else ~lo, qb, jnp.zeros_like(qb))
            s = lax.dot_general(qm, kb, NT_DIMS, preferred_element_type=F32)
            s = s + bias_ref[branch, hh]
            if edge:
                s = jnp.where(in_seq, s, NEG_INF)
            m = jnp.max(s, axis=-1, keepdims=True)
            p = jnp.exp(s - m)
            l = jnp.sum(p, axis=-1, keepdims=True)
            pv = jnp.dot(p.astype(BF16), vb, preferred_element_type=F32)
            res_h.append((pv, m, l))
        return tuple(jnp.where(lo, a, b) for a, b in zip(res_h[0], res_h[1]))

    def strided_branch(dil, acc_ref, m_ref, l_ref):
        for u in range(SUPER // TQ):
            res = u % dil
            qs = (u // dil) * TQ
            acc, m, l = branch_unit(dil, res, qs)
            rows = pl.ds(dil * qs + res, TQ, stride=dil)
            acc_ref[rows, :] = acc
            m_ref[rows, :] = m
            l_ref[rows, :] = l

    strided_branch(16, acc16, m16, l16)
    strided_branch(4, acc4, m4, l4)

    for u in range(SUPER // TQ):
        qs = u * TQ
        a1, m1, l1 = branch_unit(1, 0, qs)
        rows = pl.ds(qs, TQ)
        m_4, m_16 = m4[rows, :], m16[rows, :]
        m_all = jnp.maximum(jnp.maximum(m1, m_4), m_16)
        e1 = jnp.exp(m1 - m_all)
        e4 = jnp.exp(m_4 - m_all)
        e16 = jnp.exp(m_16 - m_all)
        num = e1 * a1 + e4 * acc4[rows, :] + e16 * acc16[rows, :]
        den = e1 * l1 + e4 * l4[rows, :] + e16 * l16[rows, :]
        o_ref[rows, :] = num / den


def _attn_call(q, k, v, bias):
    b, t, _ = q.shape
    nsb = t // SUPER
    nkb = t // HALO
    per = SUPER // HALO
    cur = lambda bb, s, j: (bb, s, j)
    prv = lambda bb, s, j: (bb, jnp.maximum(s * per - 1, 0), j)
    nxt = lambda bb, s, j: (bb, jnp.minimum((s + 1) * per, nkb - 1), j)
    blk = lambda rows, f: pl.BlockSpec((None, rows, LANES), f)
    bias_spec = pl.BlockSpec((None,) + bias.shape[1:], lambda bb, s, j: (j, 0, 0, 0, 0))
    stat = pltpu.VMEM((SUPER, LANES), F32)
    kv = pltpu.VMEM((SUPER + 2 * HALO, LANES), F32)
    return pl.pallas_call(
        _attn_kernel,
        grid=(b, nsb, B_WIDTH // LANES),
        in_specs=[bias_spec, blk(SUPER, cur), blk(HALO, prv), blk(SUPER, cur), blk(HALO, nxt),
                  blk(HALO, prv), blk(SUPER, cur), blk(HALO, nxt)],
        out_specs=blk(SUPER, cur),
        out_shape=jax.ShapeDtypeStruct((b, t, B_WIDTH), F32),
        scratch_shapes=[stat, kv, kv, stat, stat, stat, stat, stat, stat],
        compiler_params=pltpu.CompilerParams(dimension_semantics=("parallel", "parallel", "parallel"),
                                             vmem_limit_bytes=VMEM_LIMIT),
        name="attention",
    )(bias, q, k, k, k, v, v, v)


def _out_kernel(x_ref, ya_ref, att_ref, gnb_ref, wout_ref, gffn_ref, wr_ref, br_ref, tri_ref,
                x2_ref, ei_ref, gc_ref, cnt_ref, base_ref):
    tm = tri_ref.shape[0]

    @pl.when(pl.program_id(0) == 0)
    def _():
        base_ref[...] = jnp.zeros_like(base_ref)

    base_full = base_ref[...]
    for h in range(x_ref.shape[0] // tm):
        base_full = _route_sub_tile(h * tm, tm, base_full, x_ref, ya_ref, att_ref, gnb_ref, wout_ref, gffn_ref,
                                    wr_ref, br_ref, tri_ref, x2_ref, ei_ref, gc_ref)
    base_ref[...] = base_full
    cnt_ref[...] = base_full


def _route_sub_tile(r0, tm, base_full, x_ref, ya_ref, att_ref, gnb_ref, wout_ref, gffn_ref, wr_ref, br_ref, tri_ref,
                    x2_ref, ei_ref, gc_ref):
    rows = pl.ds(r0, tm)
    yb = _rms(att_ref[rows, :], gnb_ref[...]).astype(BF16)
    a = jnp.concatenate([ya_ref[rows, :], yb], axis=1)
    x2 = x_ref[rows, :] + jnp.dot(a, wout_ref[...], preferred_element_type=F32)
    x2_ref[rows, :] = x2
    xn = _rms(x2, gffn_ref[...])

    lg = lax.dot_general(wr_ref[...], xn.astype(BF16), NT_DIMS, preferred_element_type=F32) + br_ref[...]
    e_log = lg[0:N_EXPERTS]
    g_log = lg[N_EXPERTS:N_EXPERTS + N_GROUPS]
    r4 = lax.broadcasted_iota(jnp.int32, (N_GROUPS, tm), 0).astype(F32)
    g_max = jnp.max(g_log, axis=0, keepdims=True)
    g_sel = jnp.min(jnp.where(g_log == g_max, r4, float(N_GROUPS)), axis=0, keepdims=True)
    p_group = 1.0 / jnp.sum(jnp.exp(g_log - g_max), axis=0, keepdims=True)
    e_sel = jnp.zeros((EXPERTS_PER_GROUP, tm), F32)
    for g in range(N_GROUPS):
        e_sel = jnp.where(g_sel == float(g), e_log[g * EXPERTS_PER_GROUP:(g + 1) * EXPERTS_PER_GROUP], e_sel)
    r8 = lax.broadcasted_iota(jnp.int32, (EXPERTS_PER_GROUP, tm), 0).astype(F32)
    v1 = jnp.max(e_sel, axis=0, keepdims=True)
    i1 = jnp.min(jnp.where(e_sel == v1, r8, float(EXPERTS_PER_GROUP)), axis=0, keepdims=True)
    e_rest = jnp.where(r8 == i1, -jnp.inf, e_sel)
    v2 = jnp.max(e_rest, axis=0, keepdims=True)
    i2 = jnp.min(jnp.where(e_rest == v2, r8, float(EXPERTS_PER_GROUP)), axis=0, keepdims=True)
    d = jnp.exp(v2 - v1)
    gate1 = p_group * (1.0 / (1.0 + d))
    gate2 = p_group * (d / (1.0 + d))
    eid1 = g_sel * float(EXPERTS_PER_GROUP) + i1
    eid2 = g_sel * float(EXPERTS_PER_GROUP) + i2

    r32 = lax.broadcasted_iota(jnp.int32, (N_EXPERTS, tm), 0).astype(F32)
    oh1 = r32 == eid1
    oh2 = r32 == eid2
    oh1f = jnp.where(oh1, 1.0, 0.0)
    oh2f = jnp.where(oh2, 1.0, 0.0)
    tri = tri_ref[...]
    pre1 = jnp.dot(oh1f.astype(BF16), tri, preferred_element_type=F32)
    pre2 = jnp.dot(oh2f.astype(BF16), tri, preferred_element_type=F32)
    tot1 = jnp.sum(oh1f, axis=1, keepdims=True)
    tot2 = jnp.sum(oh2f, axis=1, keepdims=True)
    base = base_full[:, 0:1]
    rank1 = jnp.sum(jnp.where(oh1, base + pre1, 0.0), axis=0, keepdims=True)
    rank2 = jnp.sum(jnp.where(oh2, base + tot1 + pre2, 0.0), axis=0, keepdims=True)
    ei_ref[:, rows] = jnp.concatenate([eid1, eid2, rank1, rank2], axis=0).astype(jnp.int32)
    r128 = lax.broadcasted_iota(jnp.int32, (LANES, tm), 0)
    gates_rows = jnp.where(r128 == 0, gate1, jnp.where(r128 == 1, gate2, 0.0))
    gc_ref[rows, :] = gates_rows.T
    return base_full + tot1 + tot2


def _out_call(x2d, ya, att, gnb, wout, gffn, wr, br, tri):
    n = x2d.shape[0]
    tm = TM_OUT
    full = lambda shape: pl.BlockSpec(shape, lambda i: (0,) * len(shape))
    tok = lambda w: pl.BlockSpec((tm, w), lambda i: (i, 0))
    return pl.pallas_call(
        _out_kernel,
        grid=(n // tm,),
        in_specs=[tok(D_MODEL), tok(A_WIDTH), tok(B_WIDTH), full((1, B_WIDTH)), full((D_MODEL, D_MODEL)),
                  full((1, D_MODEL)), full((ROUTER_ROWS, D_MODEL)), full((ROUTER_ROWS, 1)), full(tri.shape)],
        out_specs=[tok(D_MODEL), pl.BlockSpec((4, tm), lambda i: (0, i)), tok(LANES),
                   full((N_EXPERTS, LANES))],
        out_shape=[jax.ShapeDtypeStruct((n, D_MODEL), F32),
                   jax.ShapeDtypeStruct((4, n), jnp.int32), jax.ShapeDtypeStruct((n, LANES), F32),
                   jax.ShapeDtypeStruct((N_EXPERTS, LANES), F32)],
        scratch_shapes=[pltpu.VMEM((N_EXPERTS, LANES), F32)],
        compiler_params=pltpu.CompilerParams(dimension_semantics=("arbitrary",), vmem_limit_bytes=VMEM_LIMIT),
        name="out_router",
    )(x2d, ya, att, gnb, wout, gffn, wr, br, tri)


def _row_copy(src, s, dst, d, sem):
    return pltpu.make_async_copy(src.at[pl.ds(s, 1), :], dst.at[pl.ds(d, 1), :], sem)


def _for_each_row(tm, fn):
    if MOVE_UNROLL >= tm:
        for t in range(tm):
            fn(t)
        return

    def body(g, c):
        for i in range(MOVE_UNROLL):
            fn(g * MOVE_UNROLL + i)
        return c

    lax.fori_loop(0, tm // MOVE_UNROLL, body, 0)


def _dispatch_kernel(zb_ref, zon_ref, d0_ref, d1_ref, x2_ref, gffn_ref, rows_ref, xn_buf, zero_ref, sems, zsem):
    tm = x2_ref.shape[0]
    step = pl.program_id(0)
    slot = step % 2
    xn_ref = xn_buf.at[slot]
    sem = sems.at[slot]
    xn_ref[...] = _rms(x2_ref[...], gffn_ref[...])

    @pl.when(pl.program_id(0) == 0)
    def _():
        zero_ref[...] = jnp.zeros_like(zero_ref)

        def zero_copy(i):
            start = pl.multiple_of(zb_ref[i] * ROW_BLOCK, ROW_BLOCK)
            return pltpu.make_async_copy(zero_ref, rows_ref.at[pl.ds(start, ROW_BLOCK), :], zsem)

        def start(i, c):
            @pl.when(zon_ref[i] == 1)
            def _():
                zero_copy(i).start()
            return c

        def wait(i, c):
            @pl.when(zon_ref[i] == 1)
            def _():
                zero_copy(i).wait()
            return c

        lax.fori_loop(0, 2 * N_EXPERTS, start, 0)
        lax.fori_loop(0, 2 * N_EXPERTS, wait, 0)

    def issue(t):
        _row_copy(xn_ref, t, rows_ref, d0_ref[t], sem).start(priority=0)
        _row_copy(xn_ref, t, rows_ref, d1_ref[t], sem).start(priority=1)

    _for_each_row(tm, issue)

    def wait_tile(s):
        tile = pltpu.make_async_copy(xn_buf.at[s], rows_ref.at[pl.ds(0, tm), :], sems.at[s])
        tile.wait()
        tile.wait()

    @pl.when(step > 0)
    def _():
        wait_tile(1 - slot)

    @pl.when(step == pl.num_programs(0) - 1)
    def _():
        wait_tile(slot)


def _dispatch_call(zero_blocks, zero_on, dest0, dest1, x2, gffn, n_rows):
    n = x2.shape[0]
    tm = TM_MOVE
    idx = pl.BlockSpec((tm,), lambda i, zb, zon: (i,), memory_space=pltpu.SMEM)
    grid_spec = pltpu.PrefetchScalarGridSpec(
        num_scalar_prefetch=2,
        grid=(n // tm,),
        in_specs=[idx, idx, pl.BlockSpec((tm, D_MODEL), lambda i, zb, zon: (i, 0)),
                  pl.BlockSpec((1, D_MODEL), lambda i, zb, zon: (0, 0))],
        out_specs=pl.BlockSpec(memory_space=pl.ANY),
        scratch_shapes=[pltpu.VMEM((2, tm, D_MODEL), F32), pltpu.VMEM((ROW_BLOCK, D_MODEL), F32),
                        pltpu.SemaphoreType.DMA((2,)), pltpu.SemaphoreType.DMA],
    )
    return pl.pallas_call(
        _dispatch_kernel,
        grid_spec=grid_spec,
        out_shape=jax.ShapeDtypeStruct((n_rows, D_MODEL), F32),
        compiler_params=pltpu.CompilerParams(dimension_semantics=("arbitrary",), has_side_effects=True,
                                             disable_bounds_checks=True),
        name="dispatch",
    )(zero_blocks, zero_on, dest0, dest1, x2, gffn)


def _expert_kernel(be_ref, bv_ref, bf_ref, x_ref, wg_ref, wu_ref, wd_ref, y_ref, wg_s, wu_s, wd_s):
    del be_ref
    b = pl.program_id(0)
    valid = bv_ref[b]

    @pl.when(bf_ref[b] == 1)
    def _():
        wg_s[...] = wg_ref[...].astype(BF16)
        wu_s[...] = wu_ref[...].astype(BF16)
        wd_s[...] = wd_ref[...].astype(BF16)

    @pl.when(valid > 0)
    def _():
        xb = x_ref[...].astype(BF16)
        g = jnp.dot(xb, wg_s[...], preferred_element_type=F32)
        u = jnp.dot(xb, wu_s[...], preferred_element_type=F32)
        h = (jax.nn.silu(g) * u).astype(BF16)
        y_ref[...] = jnp.dot(h, wd_s[...], preferred_element_type=F32)

    @pl.when(valid == 0)
    def _():
        y_ref[...] = jnp.zeros_like(y_ref)


def _expert_call(block_e, block_valid, block_first, x_rows, wg, wu, wd):
    n_rows = x_rows.shape[0]
    nb = n_rows // ROW_BLOCK
    weight = lambda shape: pl.BlockSpec((None,) + shape, lambda b, be, bv, bf: (be[b], 0, 0))
    grid_spec = pltpu.PrefetchScalarGridSpec(
        num_scalar_prefetch=3,
        grid=(nb,),
        in_specs=[pl.BlockSpec((ROW_BLOCK, D_MODEL), lambda b, be, bv, bf: (b, 0)),
                  weight((D_MODEL, D_EXPERT)), weight((D_MODEL, D_EXPERT)), weight((D_EXPERT, D_MODEL))],
        out_specs=pl.BlockSpec((ROW_BLOCK, D_MODEL), lambda b, be, bv, bf: (b, 0)),
        scratch_shapes=[pltpu.VMEM((D_MODEL, D_EXPERT), BF16), pltpu.VMEM((D_MODEL, D_EXPERT), BF16),
                        pltpu.VMEM((D_EXPERT, D_MODEL), BF16)],
    )
    return pl.pallas_call(
        _expert_kernel,
        grid_spec=grid_spec,
        out_shape=jax.ShapeDtypeStruct((n_rows, D_MODEL), F32),
        compiler_params=pltpu.CompilerParams(dimension_semantics=("arbitrary",), vmem_limit_bytes=VMEM_LIMIT),
        name="experts",
    )(block_e, block_valid, block_first, x_rows, wg, wu, wd)


def _combine_kernel(d0_ref, d1_ref, n0_ref, n1_ref, x2_ref, gc_ref, y_ref, o_ref, y0_buf, y1_buf, sems):
    tm = x2_ref.shape[0]
    step = pl.program_id(0)
    slot = step % 2

    def gather(i0_ref, i1_ref, s):
        def issue(t):
            _row_copy(y_ref, i0_ref[t], y0_buf.at[s], t, sems.at[s]).start(priority=0)
            _row_copy(y_ref, i1_ref[t], y1_buf.at[s], t, sems.at[s]).start(priority=1)

        _for_each_row(tm, issue)

    @pl.when(step == 0)
    def _():
        gather(d0_ref, d1_ref, slot)

    @pl.when(step < pl.num_programs(0) - 1)
    def _():
        gather(n0_ref, n1_ref, 1 - slot)

    pltpu.make_async_copy(y_ref.at[pl.ds(0, tm), :], y0_buf.at[slot], sems.at[slot]).wait()
    pltpu.make_async_copy(y_ref.at[pl.ds(0, tm), :], y1_buf.at[slot], sems.at[slot]).wait()
    gc = gc_ref[...]
    o_ref[...] = x2_ref[...] + (gc[:, 0:1] * y0_buf[slot] + gc[:, 1:2] * y1_buf[slot])


def _combine_call(dest0, dest1, x2, gc, y_rows):
    n = x2.shape[0]
    tm = TM_MOVE
    last = n // tm - 1
    idx = pl.BlockSpec((tm,), lambda i: (i,), memory_space=pltpu.SMEM)
    idx_next = pl.BlockSpec((tm,), lambda i: (jnp.minimum(i + 1, last),), memory_space=pltpu.SMEM)
    return pl.pallas_call(
        _combine_kernel,
        grid=(n // tm,),
        in_specs=[idx, idx, idx_next, idx_next, pl.BlockSpec((tm, D_MODEL), lambda i: (i, 0)),
                  pl.BlockSpec((tm, LANES), lambda i: (i, 0)), pl.BlockSpec(memory_space=pl.ANY)],
        out_specs=pl.BlockSpec((tm, D_MODEL), lambda i: (i, 0)),
        out_shape=jax.ShapeDtypeStruct((n, D_MODEL), F32),
        scratch_shapes=[pltpu.VMEM((2, tm, D_MODEL), F32), pltpu.VMEM((2, tm, D_MODEL), F32),
                        pltpu.SemaphoreType.DMA((2,))],
        compiler_params=pltpu.CompilerParams(dimension_semantics=("arbitrary",), vmem_limit_bytes=VMEM_LIMIT,
                                             disable_bounds_checks=True),
        name="combine",
    )(dest0, dest1, dest0, dest1, x2, gc, y_rows)


def _prepare(norm_mix, w_in, a_v_norm, a_spatial_w, a_spatial_b, q_norm, k_norm, out_norm_a, out_norm_b, w_out,
             norm_ffn, w_router_group, b_router_group, w_router_expert, b_router_expert,
             w_expert_gate, w_expert_up, w_expert_down):
    ch = jnp.arange(A_WIDTH) // GROUP_DIM
    pad = ROUTER_ROWS - N_EXPERTS - N_GROUPS
    dist = jnp.abs(jnp.arange(TK)[None, :] - jnp.arange(TQ)[:, None] - HALF).astype(F32)
    slope = 2.0 ** (-8.0 * (jnp.arange(HEADS, dtype=F32) + 1.0) / HEADS)
    dil = jnp.asarray(DILATIONS, F32)
    abias = jnp.where(dist <= HALF, -slope[None, :, None, None] * (dist * dil[:, None, None, None]), NEG_INF)
    abias = abias.reshape(len(DILATIONS), B_WIDTH // LANES, 2, TQ, TK).transpose(1, 0, 2, 3, 4)
    return dict(
        gmix=norm_mix.reshape(1, D_MODEL),
        win=w_in.astype(BF16),
        bd=(ch[:, None] == ch[None, :]).astype(BF16),
        avn=a_v_norm.reshape(1, A_WIDTH),
        wcat=jnp.concatenate([a_spatial_w[0::2], a_spatial_w[1::2]], axis=2).astype(BF16),
        bias=jnp.repeat(a_spatial_b.T, GROUP_DIM, axis=1),
        gq=(jnp.tile(q_norm, HEADS) * (HEAD_DIM ** -0.5)).reshape(1, B_WIDTH),
        gk=jnp.tile(k_norm, HEADS).reshape(1, B_WIDTH),
        gna=out_norm_a.reshape(1, A_WIDTH),
        gnb=out_norm_b.reshape(1, B_WIDTH),
        abias=abias,
        wout=w_out.astype(BF16),
        gffn=norm_ffn.reshape(1, D_MODEL),
        wr=jnp.concatenate([w_router_expert.T, w_router_group.T, jnp.zeros((pad, D_MODEL), F32)], axis=0).astype(BF16),
        br=jnp.concatenate([b_router_expert, b_router_group, jnp.zeros((pad,), F32)]).reshape(ROUTER_ROWS, 1),
        tri=(jnp.arange(SUB_OUT)[:, None] < jnp.arange(SUB_OUT)[None, :]).astype(BF16),
        wg=w_expert_gate,
        wu=w_expert_up,
        wd=w_expert_down,
    )


def _layer(x, p):
    b, t, _ = x.shape
    n = b * t
    x2d = x.reshape(n, D_MODEL)
    ya, q, k, v = _proj_call(x2d, p["gmix"], p["win"], p["bd"], p["avn"], p["wcat"], p["bias"], p["gq"], p["gk"], p["gna"])
    q, k, v = (a.reshape(b, t, B_WIDTH) for a in (q, k, v))
    att = _attn_call(q, k, v, p["abias"]).reshape(n, B_WIDTH)
    x2, ei, gc, cnt = _out_call(x2d, ya, att, p["gnb"], p["wout"], p["gffn"], p["wr"], p["br"], p["tri"])

    counts = cnt[:, 0].astype(jnp.int32)
    pcounts = (counts + ROW_BLOCK - 1) // ROW_BLOCK * ROW_BLOCK
    pends = jnp.cumsum(pcounts)
    pstarts = pends - pcounts
    nb = (2 * n) // ROW_BLOCK + N_EXPERTS
    starts = jnp.arange(nb, dtype=jnp.int32) * ROW_BLOCK
    block_e = jnp.minimum(jnp.sum((pends[None, :] <= starts[:, None]).astype(jnp.int32), axis=1), N_EXPERTS - 1)
    overlap = (jnp.minimum(starts[:, None] + ROW_BLOCK, (pstarts + counts)[None, :])
               - jnp.maximum(starts[:, None], pstarts[None, :]))
    block_valid = jnp.sum(jnp.clip(overlap, 0, ROW_BLOCK), axis=1)
    block_first = jnp.concatenate([jnp.ones((1,), jnp.int32), (block_e[1:] != block_e[:-1]).astype(jnp.int32)])
    expert_ids = jnp.arange(N_EXPERTS, dtype=jnp.int32)[:, None]
    row_start = lambda eid: jnp.sum(jnp.where(eid[None, :] == expert_ids, pstarts[:, None], 0), axis=0)
    dest0 = row_start(ei[0]) + ei[2]
    dest1 = row_start(ei[1]) + ei[3]
    used = pends[N_EXPERTS - 1] // ROW_BLOCK
    tail = used + jnp.arange(N_EXPERTS, dtype=jnp.int32)
    zero_blocks = jnp.concatenate([jnp.maximum(pends // ROW_BLOCK - 1, 0), jnp.minimum(tail, nb - 1)]).astype(jnp.int32)
    zero_on = jnp.concatenate([pcounts > 0, tail < nb]).astype(jnp.int32)

    x_rows = _dispatch_call(zero_blocks, zero_on, dest0, dest1, x2, p["gffn"], nb * ROW_BLOCK)
    y_rows = _expert_call(block_e, block_valid, block_first, x_rows, p["wg"], p["wu"], p["wd"])
    out = _combine_call(dest0, dest1, x2, gc, y_rows)
    return out.reshape(b, t, D_MODEL)


def kernel(x_prompt, x_sample, norm_mix, w_in, a_v_norm, a_spatial_w, a_spatial_b, q_norm, k_norm, out_norm_a,
           out_norm_b, w_out, norm_ffn, w_router_group, b_router_group, w_router_expert, b_router_expert,
           w_expert_gate, w_expert_up, w_expert_down):
    depth = norm_mix.shape[0]
    layers = [
        _prepare(norm_mix[l], w_in[l], a_v_norm[l], a_spatial_w[l], a_spatial_b[l], q_norm[l], k_norm[l],
                 out_norm_a[l], out_norm_b[l], w_out[l], norm_ffn[l], w_router_group[l], b_router_group[l],
                 w_router_expert[l], b_router_expert[l], w_expert_gate[l], w_expert_up[l], w_expert_down[l])
        for l in range(depth)
    ]

    def run(x):
        for p in layers:
            x = _layer(x, p)
        return x

    return (run(x_prompt), run(x_sample))
```

```python
import jax
import jax.numpy as jnp
from jax import lax
from jax.experimental import pallas as pl
from jax.experimental.pallas import tpu as pltpu

D_MODEL = 1024
A_WIDTH = 512
B_WIDTH = 512
IN_WIDTH = 2 * A_WIDTH + 3 * B_WIDTH
A_GROUPS = 8
GROUP_DIM = 64
CHUNK = 128
HEADS = 8
HEAD_DIM = 64
DILATIONS = (1, 4, 16)
HALF = 64
N_GROUPS = 4
EXPERTS_PER_GROUP = 8
N_EXPERTS = 32
D_EXPERT = 512
ROW_BLOCK = 512
X_BUFFERS = 3
EPS = 1e-6
NEG_INF = -1e30

LANES = 128
ROUTER_ROWS = 48
TM_PROJ = 512
TM_OUT = 1024
SUB_OUT = 512
TM_MOVE = 512
MOVE_UNROLL = 512
TQ = 128
TK = TQ + 2 * HALF
SUPER = TQ * max(DILATIONS)
HALO = HALF * max(DILATIONS)
VMEM_LIMIT = 48 * 1024 * 1024

F32 = jnp.float32
BF16 = jnp.bfloat16
NT_DIMS = (((1,), (1,)), ((), ()))


def _rms(x, gain):
    return x * lax.rsqrt(jnp.mean(x * x, axis=-1, keepdims=True) + EPS) * gain


def _proj_kernel(x_ref, gmix_ref, win_ref, bd_ref, avn_ref, wcat_ref, bias_ref, gq_ref, gk_ref, gna_ref,
                 ya_ref, q_ref, k_ref, v_ref):
    h = _rms(x_ref[...], gmix_ref[...])
    proj = jnp.dot(h.astype(BF16), win_ref[...], preferred_element_type=F32)
    pu = proj[:, 0:A_WIDTH]
    pv = proj[:, A_WIDTH:2 * A_WIDTH]
    q = proj[:, 2 * A_WIDTH:2 * A_WIDTH + B_WIDTH]
    k = proj[:, 2 * A_WIDTH + B_WIDTH:2 * A_WIDTH + 2 * B_WIDTH]
    v = proj[:, 2 * A_WIDTH + 2 * B_WIDTH:]
    bd = bd_ref[...]

    def group_norm(t, gain):
        ms = jnp.dot((t * t).astype(BF16), bd, preferred_element_type=F32) * (1.0 / GROUP_DIM)
        return t * lax.rsqrt(ms + EPS) * gain

    u = jax.nn.gelu(pu)
    vn = group_norm(jax.nn.gelu(pv), avn_ref[...]).astype(BF16)
    lane = lax.broadcasted_iota(jnp.int32, (CHUNK, LANES), 1)
    lo = lane < GROUP_DIM
    zero = jnp.zeros((CHUNK, LANES), BF16)
    chunks = []
    for c in range(x_ref.shape[0] // CHUNK):
        blks = []
        for j in range(A_WIDTH // LANES):
            vb = vn[c * CHUNK:(c + 1) * CHUNK, j * LANES:(j + 1) * LANES]
            rhs = jnp.concatenate([jnp.where(lo, vb, zero), jnp.where(lo, zero, vb)], axis=0)
            blks.append(jnp.dot(wcat_ref[j], rhs, preferred_element_type=F32))
        chunks.append(jnp.concatenate(blks, axis=1) + bias_ref[...])
    mixed = jnp.concatenate(chunks, axis=0)
    ya_ref[...] = _rms(u * mixed, gna_ref[...]).astype(BF16)
    q_ref[...] = group_norm(q, gq_ref[...]).astype(BF16)
    k_ref[...] = group_norm(k, gk_ref[...]).astype(BF16)
    v_ref[...] = v.astype(BF16)


def _proj_call(x2d, gmix, win, bd, avn, wcat, bias, gq, gk, gna):
    n = x2d.shape[0]
    tm = TM_PROJ
    full = lambda shape: pl.BlockSpec(shape, lambda i: (0,) * len(shape))
    tok = lambda w: pl.BlockSpec((tm, w), lambda i: (i, 0))
    return pl.pallas_call(
        _proj_kernel,
        grid=(n // tm,),
        in_specs=[tok(D_MODEL), full((1, D_MODEL)), full((D_MODEL, IN_WIDTH)), full((A_WIDTH, A_WIDTH)),
                  full((1, A_WIDTH)), full((A_WIDTH // LANES, CHUNK, 2 * CHUNK)), full((CHUNK, A_WIDTH)),
                  full((1, B_WIDTH)), full((1, B_WIDTH)), full((1, A_WIDTH))],
        out_specs=[tok(A_WIDTH), tok(B_WIDTH), tok(B_WIDTH), tok(B_WIDTH)],
        out_shape=[jax.ShapeDtypeStruct((n, A_WIDTH), BF16)] + [jax.ShapeDtypeStruct((n, B_WIDTH), BF16)] * 3,
        compiler_params=pltpu.CompilerParams(dimension_semantics=("parallel",), vmem_limit_bytes=VMEM_LIMIT),
        name="proj_gating",
    )(x2d, gmix, win, bd, avn, wcat, bias, gq, gk, gna)


def _attn_kernel(bias_ref, q_ref, kp_ref, kc_ref, kn_ref, vp_ref, vc_ref, vn_ref, o_ref,
                 qf, kf, vf, acc16, m16, l16, acc4, m4, l4):
    sb = pl.program_id(1)

    qf[...] = q_ref[...].astype(F32)
    kf[0:HALO, :] = kp_ref[...].astype(F32)
    kf[HALO:HALO + SUPER, :] = kc_ref[...].astype(F32)
    kf[HALO + SUPER:, :] = kn_ref[...].astype(F32)
    vf[0:HALO, :] = vp_ref[...].astype(F32)
    vf[HALO:HALO + SUPER, :] = vc_ref[...].astype(F32)
    vf[HALO + SUPER:, :] = vn_ref[...].astype(F32)

    lane = lax.broadcasted_iota(jnp.int32, (TQ, LANES), 1)
    lo = lane < HEAD_DIM

    def branch_unit(dil, res, qs):
        n_sub = SUPER // dil
        q_start = dil * qs + res
        k_start = HALO + dil * (qs - HALF) + res
        if dil == 1:
            qb = qf[pl.ds(q_start, TQ), :]
            kb = kf[pl.ds(k_start, TK), :]
            vb = vf[pl.ds(k_start, TK), :]
        else:
            qb = qf[pl.ds(q_start, TQ, stride=dil), :]
            kb = kf[pl.ds(k_start, TK, stride=dil), :]
            vb = vf[pl.ds(k_start, TK, stride=dil), :]
        qb = qb.astype(BF16)
        kb = kb.astype(BF16)
        vb = vb.astype(BF16)
        branch = DILATIONS.index(dil)
        edge = qs == 0 or qs + TQ == n_sub
        if edge:
            col = lax.broadcasted_iota(jnp.int32, (1, TK), 1)
            c_lo = jnp.where(sb == 0, HALF - qs, 0)
            c_hi = jnp.where(sb == pl.num_programs(1) - 1, n_sub - qs + HALF, TK)
            in_seq = (col >= c_lo) & (col < c_hi)
        res_h = []
        for hh in range(2):
            qm = jnp.where(lo if hh == 0 else ~lo, qb, jnp.zeros_like(qb))
            s = lax.dot_general(qm, kb, NT_DIMS, preferred_element_type=F32)
            s = s + bias_ref[branch, hh]
            if edge:
                s = jnp.where(in_seq, s, NEG_INF)
            m = jnp.max(s, axis=-1, keepdims=True)
            p = jnp.exp(s - m)
            l = jnp.sum(p, axis=-1, keepdims=True)
            pv = jnp.dot(p.astype(BF16), vb, preferred_element_type=F32)
            res_h.append((pv, m, l))
        return tuple(jnp.where(lo, a, b) for a, b in zip(res_h[0], res_h[1]))

    def strided_branch(dil, acc_ref, m_ref, l_ref):
        for u in range(SUPER // TQ):
            res = u % dil
            qs = (u // dil) * TQ
            acc, m, l = branch_unit(dil, res, qs)
            rows = pl.ds(dil * qs + res, TQ, stride=dil)
            acc_ref[rows, :] = acc
            m_ref[rows, :] = m
            l_ref[rows, :] = l

    strided_branch(16, acc16, m16, l16)
    strided_branch(4, acc4, m4, l4)

    for u in range(SUPER // TQ):
        qs = u * TQ
        a1, m1, l1 = branch_unit(1, 0, qs)
        rows = pl.ds(qs, TQ)
        m_4, m_16 = m4[rows, :], m16[rows, :]
        m_all = jnp.maximum(jnp.maximum(m1, m_4), m_16)
        e1 = jnp.exp(m1 - m_all)
        e4 = jnp.exp(m_4 - m_all)
        e16 = jnp.exp(m_16 - m_all)
        num = e1 * a1 + e4 * acc4[rows, :] + e16 * acc16[rows, :]
        den = e1 * l1 + e4 * l4[rows, :] + e16 * l16[rows, :]
        o_ref[rows, :] = num / den


def _attn_call(q, k, v, bias):
    b, t, _ = q.shape
    nsb = t // SUPER
    nkb = t // HALO
    per = SUPER // HALO
    cur = lambda bb, s, j: (bb, s, j)
    prv = lambda bb, s, j: (bb, jnp.maximum(s * per - 1, 0), j)
    nxt = lambda bb, s, j: (bb, jnp.minimum((s + 1) * per, nkb - 1), j)
    blk = lambda rows, f: pl.BlockSpec((None, rows, LANES), f)
    bias_spec = pl.BlockSpec((None,) + bias.shape[1:], lambda bb, s, j: (j, 0, 0, 0, 0))
    stat = pltpu.VMEM((SUPER, LANES), F32)
    kv = pltpu.VMEM((SUPER + 2 * HALO, LANES), F32)
    return pl.pallas_call(
        _attn_kernel,
        grid=(b, nsb, B_WIDTH // LANES),
        in_specs=[bias_spec, blk(SUPER, cur), blk(HALO, prv), blk(SUPER, cur), blk(HALO, nxt),
                  blk(HALO, prv), blk(SUPER, cur), blk(HALO, nxt)],
        out_specs=blk(SUPER, cur),
        out_shape=jax.ShapeDtypeStruct((b, t, B_WIDTH), F32),
        scratch_shapes=[stat, kv, kv, stat, stat, stat, stat, stat, stat],
        compiler_params=pltpu.CompilerParams(dimension_semantics=("parallel", "parallel", "parallel"),
                                             vmem_limit_bytes=VMEM_LIMIT),
        name="attention",
    )(bias, q, k, k, k, v, v, v)


def _out_kernel(x_ref, ya_ref, att_ref, gnb_ref, wout_ref, gffn_ref, wr_ref, br_ref, tri_ref,
                x2_ref, ei_ref, gc_ref, cnt_ref, base_ref):
    tm = tri_ref.shape[0]

    @pl.when(pl.program_id(0) == 0)
    def _():
        base_ref[...] = jnp.zeros_like(base_ref)

    base_full = base_ref[...]
    for h in range(x_ref.shape[0] // tm):
        base_full = _route_sub_tile(h * tm, tm, base_full, x_ref, ya_ref, att_ref, gnb_ref, wout_ref, gffn_ref,
                                    wr_ref, br_ref, tri_ref, x2_ref, ei_ref, gc_ref)
    base_ref[...] = base_full
    cnt_ref[...] = base_full


def _route_sub_tile(r0, tm, base_full, x_ref, ya_ref, att_ref, gnb_ref, wout_ref, gffn_ref, wr_ref, br_ref, tri_ref,
                    x2_ref, ei_ref, gc_ref):
    rows = pl.ds(r0, tm)
    yb = _rms(att_ref[rows, :], gnb_ref[...]).astype(BF16)
    a = jnp.concatenate([ya_ref[rows, :], yb], axis=1)
    x2 = x_ref[rows, :] + jnp.dot(a, wout_ref[...], preferred_element_type=F32)
    x2_ref[rows, :] = x2
    xn = _rms(x2, gffn_ref[...])

    lg = lax.dot_general(wr_ref[...], xn.astype(BF16), NT_DIMS, preferred_element_type=F32) + br_ref[...]
    e_log = lg[0:N_EXPERTS]
    g_log = lg[N_EXPERTS:N_EXPERTS + N_GROUPS]
    r4 = lax.broadcasted_iota(jnp.int32, (N_GROUPS, tm), 0).astype(F32)
    g_max = jnp.max(g_log, axis=0, keepdims=True)
    g_sel = jnp.min(jnp.where(g_log == g_max, r4, float(N_GROUPS)), axis=0, keepdims=True)
    p_group = 1.0 / jnp.sum(jnp.exp(g_log - g_max), axis=0, keepdims=True)
    e_sel = jnp.zeros((EXPERTS_PER_GROUP, tm), F32)
    for g in range(N_GROUPS):
        e_sel = jnp.where(g_sel == float(g), e_log[g * EXPERTS_PER_GROUP:(g + 1) * EXPERTS_PER_GROUP], e_sel)
    r8 = lax.broadcasted_iota(jnp.int32, (EXPERTS_PER_GROUP, tm), 0).astype(F32)
    v1 = jnp.max(e_sel, axis=0, keepdims=True)
    i1 = jnp.min(jnp.where(e_sel == v1, r8, float(EXPERTS_PER_GROUP)), axis=0, keepdims=True)
    e_rest = jnp.where(r8 == i1, -jnp.inf, e_sel)
    v2 = jnp.max(e_rest, axis=0, keepdims=True)
    i2 = jnp.min(jnp.where(e_rest == v2, r8, float(EXPERTS_PER_GROUP)), axis=0, keepdims=True)
    d = jnp.exp(v2 - v1)
    gate1 = p_group * (1.0 / (1.0 + d))
    gate2 = p_group * (d / (1.0 + d))
    eid1 = g_sel * float(EXPERTS_PER_GROUP) + i1
    eid2 = g_sel * float(EXPERTS_PER_GROUP) + i2

    r32 = lax.broadcasted_iota(jnp.int32, (N_EXPERTS, tm), 0).astype(F32)
    oh1 = r32 == eid1
    oh2 = r32 == eid2
    oh1f = jnp.where(oh1, 1.0, 0.0)
    oh2f = jnp.where(oh2, 1.0, 0.0)
    tri = tri_ref[...]
    pre1 = jnp.dot(oh1f.astype(BF16), tri, preferred_element_type=F32)
    pre2 = jnp.dot(oh2f.astype(BF16), tri, preferred_element_type=F32)
    tot1 = jnp.sum(oh1f, axis=1, keepdims=True)
    tot2 = jnp.sum(oh2f, axis=1, keepdims=True)
    base = base_full[:, 0:1]
    rank1 = jnp.sum(jnp.where(oh1, base + pre1, 0.0), axis=0, keepdims=True)
    rank2 = jnp.sum(jnp.where(oh2, base + tot1 + pre2, 0.0), axis=0, keepdims=True)
    ei_ref[:, rows] = jnp.concatenate([eid1, eid2, rank1, rank2], axis=0).astype(jnp.int32)
    r128 = lax.broadcasted_iota(jnp.int32, (LANES, tm), 0)
    gates_rows = jnp.where(r128 == 0, gate1, jnp.where(r128 == 1, gate2, 0.0))
    gc_ref[rows, :] = gates_rows.T
    return base_full + tot1 + tot2


def _out_call(x2d, ya, att, gnb, wout, gffn, wr, br, tri):
    n = x2d.shape[0]
    tm = TM_OUT
    full = lambda shape: pl.BlockSpec(shape, lambda i: (0,) * len(shape))
    tok = lambda w: pl.BlockSpec((tm, w), lambda i: (i, 0))
    return pl.pallas_call(
        _out_kernel,
        grid=(n // tm,),
        in_specs=[tok(D_MODEL), tok(A_WIDTH), tok(B_WIDTH), full((1, B_WIDTH)), full((D_MODEL, D_MODEL)),
                  full((1, D_MODEL)), full((ROUTER_ROWS, D_MODEL)), full((ROUTER_ROWS, 1)), full(tri.shape)],
        out_specs=[tok(D_MODEL), pl.BlockSpec((4, tm), lambda i: (0, i)), tok(LANES),
                   full((N_EXPERTS, LANES))],
        out_shape=[jax.ShapeDtypeStruct((n, D_MODEL), F32),
                   jax.ShapeDtypeStruct((4, n), jnp.int32), jax.ShapeDtypeStruct((n, LANES), F32),
                   jax.ShapeDtypeStruct((N_EXPERTS, LANES), F32)],
        scratch_shapes=[pltpu.VMEM((N_EXPERTS, LANES), F32)],
        compiler_params=pltpu.CompilerParams(dimension_semantics=("arbitrary",), vmem_limit_bytes=VMEM_LIMIT),
        name="out_router",
    )(x2d, ya, att, gnb, wout, gffn, wr, br, tri)


def _row_copy(src, s, dst, d, sem):
    return pltpu.make_async_copy(src.at[pl.ds(s, 1), :], dst.at[pl.ds(d, 1), :], sem)


def _for_each_row(tm, fn):
    if MOVE_UNROLL >= tm:
        for t in range(tm):
            fn(t)
        return

    def body(g, c):
        for i in range(MOVE_UNROLL):
            fn(g * MOVE_UNROLL + i)
        return c

    lax.fori_loop(0, tm // MOVE_UNROLL, body, 0)


def _dispatch_kernel(zb_ref, zon_ref, d0_ref, d1_ref, x2_ref, gffn_ref, rows_ref, xn_buf, zero_ref, sems, zsem):
    tm = x2_ref.shape[0]
    step = pl.program_id(0)
    slot = step % 2
    xn_ref = xn_buf.at[slot]
    sem = sems.at[slot]
    xn_ref[...] = _rms(x2_ref[...], gffn_ref[...])

    @pl.when(pl.program_id(0) == 0)
    def _():
        zero_ref[...] = jnp.zeros_like(zero_ref)

        def zero_copy(i):
            start = pl.multiple_of(zb_ref[i] * ROW_BLOCK, ROW_BLOCK)
            return pltpu.make_async_copy(zero_ref, rows_ref.at[pl.ds(start, ROW_BLOCK), :], zsem)

        def start(i, c):
            @pl.when(zon_ref[i] == 1)
            def _():
                zero_copy(i).start()
            return c

        def wait(i, c):
            @pl.when(zon_ref[i] == 1)
            def _():
                zero_copy(i).wait()
            return c

        lax.fori_loop(0, 2 * N_EXPERTS, start, 0)
        lax.fori_loop(0, 2 * N_EXPERTS, wait, 0)

    def issue(t):
        _row_copy(xn_ref, t, rows_ref, d0_ref[t], sem).start(priority=0)
        _row_copy(xn_ref, t, rows_ref, d1_ref[t], sem).start(priority=1)

    _for_each_row(tm, issue)

    def wait_tile(s):
        tile = pltpu.make_async_copy(xn_buf.at[s], rows_ref.at[pl.ds(0, tm), :], sems.at[s])
        tile.wait()
        tile.wait()

    @pl.when(step > 0)
    def _():
        wait_tile(1 - slot)

    @pl.when(step == pl.num_programs(0) - 1)
    def _():
        wait_tile(slot)


def _dispatch_call(zero_blocks, zero_on, dest0, dest1, x2, gffn, n_rows):
    n = x2.shape[0]
    tm = TM_MOVE
    idx = pl.BlockSpec((tm,), lambda i, zb, zon: (i,), memory_space=pltpu.SMEM)
    grid_spec = pltpu.PrefetchScalarGridSpec(
        num_scalar_prefetch=2,
        grid=(n // tm,),
        in_specs=[idx, idx, pl.BlockSpec((tm, D_MODEL), lambda i, zb, zon: (i, 0)),
                  pl.BlockSpec((1, D_MODEL), lambda i, zb, zon: (0, 0))],
        out_specs=pl.BlockSpec(memory_space=pl.ANY),
        scratch_shapes=[pltpu.VMEM((2, tm, D_MODEL), F32), pltpu.VMEM((ROW_BLOCK, D_MODEL), F32),
                        pltpu.SemaphoreType.DMA((2,)), pltpu.SemaphoreType.DMA],
    )
    return pl.pallas_call(
        _dispatch_kernel,
        grid_spec=grid_spec,
        out_shape=jax.ShapeDtypeStruct((n_rows, D_MODEL), F32),
        compiler_params=pltpu.CompilerParams(dimension_semantics=("arbitrary",), has_side_effects=True,
                                             disable_bounds_checks=True),
        name="dispatch",
    )(zero_blocks, zero_on, dest0, dest1, x2, gffn)


def _expert_kernel(be_ref, bv_ref, bf_ref, x_hbm, wg_ref, wu_ref, wd_ref, y_ref, wg_s, wu_s, wd_s, x_buf, x_sems):
    del be_ref
    b = pl.program_id(0)
    nb = pl.num_programs(0)
    valid = bv_ref[b]

    def x_copy(blk):
        slot = blk % X_BUFFERS
        start = pl.multiple_of(blk * ROW_BLOCK, ROW_BLOCK)
        return pltpu.make_async_copy(x_hbm.at[pl.ds(start, ROW_BLOCK), :], x_buf.at[slot], x_sems.at[slot])

    @pl.when(b == 0)
    def _():
        for i in range(X_BUFFERS - 1):
            x_copy(i).start()

    @pl.when(b + X_BUFFERS - 1 < nb)
    def _():
        x_copy(b + X_BUFFERS - 1).start()

    x_copy(b).wait()
    x_ref = x_buf.at[b % X_BUFFERS]

    @pl.when(bf_ref[b] == 1)
    def _():
        wg_s[...] = wg_ref[...].astype(BF16)
        wu_s[...] = wu_ref[...].astype(BF16)
        wd_s[...] = wd_ref[...].astype(BF16)

    @pl.when(valid > 0)
    def _():
        xb = x_ref[...].astype(BF16)
        g = jnp.dot(xb, wg_s[...], preferred_element_type=F32)
        u = jnp.dot(xb, wu_s[...], preferred_element_type=F32)
        h = (jax.nn.silu(g) * u).astype(BF16)
        y_ref[...] = jnp.dot(h, wd_s[...], preferred_element_type=F32)

    @pl.when(valid == 0)
    def _():
        y_ref[...] = jnp.zeros_like(y_ref)


def _expert_call(block_e, block_valid, block_first, x_rows, wg, wu, wd):
    n_rows = x_rows.shape[0]
    nb = n_rows // ROW_BLOCK
    weight = lambda shape: pl.BlockSpec((None,) + shape, lambda b, be, bv, bf: (be[b], 0, 0))
    grid_spec = pltpu.PrefetchScalarGridSpec(
        num_scalar_prefetch=3,
        grid=(nb,),
        in_specs=[pl.BlockSpec(memory_space=pl.ANY),
                  weight((D_MODEL, D_EXPERT)), weight((D_MODEL, D_EXPERT)), weight((D_EXPERT, D_MODEL))],
        out_specs=pl.BlockSpec((ROW_BLOCK, D_MODEL), lambda b, be, bv, bf: (b, 0)),
        scratch_shapes=[pltpu.VMEM((D_MODEL, D_EXPERT), BF16), pltpu.VMEM((D_MODEL, D_EXPERT), BF16),
                        pltpu.VMEM((D_EXPERT, D_MODEL), BF16), pltpu.VMEM((X_BUFFERS, ROW_BLOCK, D_MODEL), F32),
                        pltpu.SemaphoreType.DMA((X_BUFFERS,))],
    )
    return pl.pallas_call(
        _expert_kernel,
        grid_spec=grid_spec,
        out_shape=jax.ShapeDtypeStruct((n_rows, D_MODEL), F32),
        compiler_params=pltpu.CompilerParams(dimension_semantics=("arbitrary",), vmem_limit_bytes=VMEM_LIMIT),
        name="experts",
    )(block_e, block_valid, block_first, x_rows, wg, wu, wd)


def _combine_kernel(d0_ref, d1_ref, n0_ref, n1_ref, x2_ref, gc_ref, y_ref, o_ref, y0_buf, y1_buf, sems):
    tm = x2_ref.shape[0]
    step = pl.program_id(0)
    slot = step % 2

    def gather(i0_ref, i1_ref, s):
        def issue(t):
            _row_copy(y_ref, i0_ref[t], y0_buf.at[s], t, sems.at[s]).start(priority=0)
            _row_copy(y_ref, i1_ref[t], y1_buf.at[s], t, sems.at[s]).start(priority=1)

        _for_each_row(tm, issue)

    @pl.when(step == 0)
    def _():
        gather(d0_ref, d1_ref, slot)

    @pl.when(step < pl.num_programs(0) - 1)
    def _():
        gather(n0_ref, n1_ref, 1 - slot)

    pltpu.make_async_copy(y_ref.at[pl.ds(0, tm), :], y0_buf.at[slot], sems.at[slot]).wait()
    pltpu.make_async_copy(y_ref.at[pl.ds(0, tm), :], y1_buf.at[slot], sems.at[slot]).wait()
    gc = gc_ref[...]
    o_ref[...] = x2_ref[...] + (gc[:, 0:1] * y0_buf[slot] + gc[:, 1:2] * y1_buf[slot])


def _combine_call(dest0, dest1, x2, gc, y_rows):
    n = x2.shape[0]
    tm = TM_MOVE
    last = n // tm - 1
    idx = pl.BlockSpec((tm,), lambda i: (i,), memory_space=pltpu.SMEM)
    idx_next = pl.BlockSpec((tm,), lambda i: (jnp.minimum(i + 1, last),), memory_space=pltpu.SMEM)
    return pl.pallas_call(
        _combine_kernel,
        grid=(n // tm,),
        in_specs=[idx, idx, idx_next, idx_next, pl.BlockSpec((tm, D_MODEL), lambda i: (i, 0)),
                  pl.BlockSpec((tm, LANES), lambda i: (i, 0)), pl.BlockSpec(memory_space=pl.ANY)],
        out_specs=pl.BlockSpec((tm, D_MODEL), lambda i: (i, 0)),
        out_shape=jax.ShapeDtypeStruct((n, D_MODEL), F32),
        scratch_shapes=[pltpu.VMEM((2, tm, D_MODEL), F32), pltpu.VMEM((2, tm, D_MODEL), F32),
                        pltpu.SemaphoreType.DMA((2,))],
        compiler_params=pltpu.CompilerParams(dimension_semantics=("arbitrary",), vmem_limit_bytes=VMEM_LIMIT,
                                             disable_bounds_checks=True),
        name="combine",
    )(dest0, dest1, dest0, dest1, x2, gc, y_rows)


def _prepare(norm_mix, w_in, a_v_norm, a_spatial_w, a_spatial_b, q_norm, k_norm, out_norm_a, out_norm_b, w_out,
             norm_ffn, w_router_group, b_router_group, w_router_expert, b_router_expert,
             w_expert_gate, w_expert_up, w_expert_down):
    ch = jnp.arange(A_WIDTH) // GROUP_DIM
    pad = ROUTER_ROWS - N_EXPERTS - N_GROUPS
    dist = jnp.abs(jnp.arange(TK)[None, :] - jnp.arange(TQ)[:, None] - HALF).astype(F32)
    slope = 2.0 ** (-8.0 * (jnp.arange(HEADS, dtype=F32) + 1.0) / HEADS)
    dil = jnp.asarray(DILATIONS, F32)
    abias = jnp.where(dist <= HALF, -slope[None, :, None, None] * (dist * dil[:, None, None, None]), NEG_INF)
    abias = abias.reshape(len(DILATIONS), B_WIDTH // LANES, 2, TQ, TK).transpose(1, 0, 2, 3, 4)
    return dict(
        gmix=norm_mix.reshape(1, D_MODEL),
        win=w_in.astype(BF16),
        bd=(ch[:, None] == ch[None, :]).astype(BF16),
        avn=a_v_norm.reshape(1, A_WIDTH),
        wcat=jnp.concatenate([a_spatial_w[0::2], a_spatial_w[1::2]], axis=2).astype(BF16),
        bias=jnp.repeat(a_spatial_b.T, GROUP_DIM, axis=1),
        gq=(jnp.tile(q_norm, HEADS) * (HEAD_DIM ** -0.5)).reshape(1, B_WIDTH),
        gk=jnp.tile(k_norm, HEADS).reshape(1, B_WIDTH),
        gna=out_norm_a.reshape(1, A_WIDTH),
        gnb=out_norm_b.reshape(1, B_WIDTH),
        abias=abias,
        wout=w_out.astype(BF16),
        gffn=norm_ffn.reshape(1, D_MODEL),
        wr=jnp.concatenate([w_router_expert.T, w_router_group.T, jnp.zeros((pad, D_MODEL), F32)], axis=0).astype(BF16),
        br=jnp.concatenate([b_router_expert, b_router_group, jnp.zeros((pad,), F32)]).reshape(ROUTER_ROWS, 1),
        tri=(jnp.arange(SUB_OUT)[:, None] < jnp.arange(SUB_OUT)[None, :]).astype(BF16),
        wg=w_expert_gate,
        wu=w_expert_up,
        wd=w_expert_down,
    )


def _layer(x, p):
    b, t, _ = x.shape
    n = b * t
    x2d = x.reshape(n, D_MODEL)
    ya, q, k, v = _proj_call(x2d, p["gmix"], p["win"], p["bd"], p["avn"], p["wcat"], p["bias"], p["gq"], p["gk"], p["gna"])
    q, k, v = (a.reshape(b, t, B_WIDTH) for a in (q, k, v))
    att = _attn_call(q, k, v, p["abias"]).reshape(n, B_WIDTH)
    x2, ei, gc, cnt = _out_call(x2d, ya, att, p["gnb"], p["wout"], p["gffn"], p["wr"], p["br"], p["tri"])

    counts = cnt[:, 0].astype(jnp.int32)
    pcounts = (counts + ROW_BLOCK - 1) // ROW_BLOCK * ROW_BLOCK
    pends = jnp.cumsum(pcounts)
    pstarts = pends - pcounts
    nb = (2 * n) // ROW_BLOCK + N_EXPERTS
    starts = jnp.arange(nb, dtype=jnp.int32) * ROW_BLOCK
    block_e = jnp.minimum(jnp.sum((pends[None, :] <= starts[:, None]).astype(jnp.int32), axis=1), N_EXPERTS - 1)
    overlap = (jnp.minimum(starts[:, None] + ROW_BLOCK, (pstarts + counts)[None, :])
               - jnp.maximum(starts[:, None], pstarts[None, :]))
    block_valid = jnp.sum(jnp.clip(overlap, 0, ROW_BLOCK), axis=1)
    block_first = jnp.concatenate([jnp.ones((1,), jnp.int32), (block_e[1:] != block_e[:-1]).astype(jnp.int32)])
    expert_ids = jnp.arange(N_EXPERTS, dtype=jnp.int32)[:, None]
    row_start = lambda eid: jnp.sum(jnp.where(eid[None, :] == expert_ids, pstarts[:, None], 0), axis=0)
    dest0 = row_start(ei[0]) + ei[2]
    dest1 = row_start(ei[1]) + ei[3]
    used = pends[N_EXPERTS - 1] // ROW_BLOCK
    tail = used + jnp.arange(N_EXPERTS, dtype=jnp.int32)
    zero_blocks = jnp.concatenate([jnp.maximum(pends // ROW_BLOCK - 1, 0), jnp.minimum(tail, nb - 1)]).astype(jnp.int32)
    zero_on = jnp.concatenate([pcounts > 0, tail < nb]).astype(jnp.int32)

    x_rows = _dispatch_call(zero_blocks, zero_on, dest0, dest1, x2, p["gffn"], nb * ROW_BLOCK)
    y_rows = _expert_call(block_e, block_valid, block_first, x_rows, p["wg"], p["wu"], p["wd"])
    out = _combine_call(dest0, dest1, x2, gc, y_rows)
    return out.reshape(b, t, D_MODEL)


def kernel(x_prompt, x_sample, norm_mix, w_in, a_v_norm, a_spatial_w, a_spatial_b, q_norm, k_norm, out_norm_a,
           out_norm_b, w_out, norm_ffn, w_router_group, b_router_group, w_router_expert, b_router_expert,
           w_expert_gate, w_expert_up, w_expert_down):
    depth = norm_mix.shape[0]
    layers = [
        _prepare(norm_mix[l], w_in[l], a_v_norm[l], a_spatial_w[l], a_spatial_b[l], q_norm[l], k_norm[l],
                 out_norm_a[l], out_norm_b[l], w_out[l], norm_ffn[l], w_router_group[l], b_router_group[l],
                 w_router_expert[l], b_router_expert[l], w_expert_gate[l], w_expert_up[l], w_expert_down[l])
        for l in range(depth)
    ]

    def run(x):
        for p in layers:
            x = _layer(x, p)
        return x

    return (run(x_prompt), run(x_sample))
```
